```python
import math
import jax, jax.numpy as jnp
from jax import lax
import numpy as np

D_MODEL = 1024
BATCH = 8
SEQ = 4096
DEPTH = 2

MIX_WIDTH = D_MODEL
DA_HEAD = 64
DA_VDIM = 2 * DA_HEAD
DA_WIDTH = MIX_WIDTH // 2
DA_HEADS = DA_WIDTH // DA_VDIM
CF_WIDTH = MIX_WIDTH // 4
CF_GROUPS = 4
CF_CONV = 31
SC_WIDTH = MIX_WIDTH // 4
SC_CONV = 3
IN_COLS = 3 * DA_WIDTH + 2 * CF_WIDTH + 3 * SC_WIDTH

Q_BLOCK = 128
RMS_EPS = 1e-6
LN_EPS = 1e-5

PEER_HEADS = 8
N_KEYS = 128
N_EXPERTS = N_KEYS * N_KEYS
PEER_QDIM = 256
PEER_TOPK = 16
PEER_CHUNK = 128

kernel_name = "hybrid_diffattn_conformer_shortconv_peer"


def rmsnorm(x, g):
    xf = x.astype(jnp.float32)
    y = xf * lax.rsqrt(jnp.mean(xf * xf, axis=-1, keepdims=True) + RMS_EPS)
    return (y * g.astype(jnp.float32)).astype(x.dtype)


def group_layernorm(x, g, b, groups):
    shp = x.shape
    xf = x.astype(jnp.float32).reshape(shp[:-1] + (groups, shp[-1] // groups))
    mu = jnp.mean(xf, axis=-1, keepdims=True)
    xc = xf - mu
    var = jnp.mean(xc * xc, axis=-1, keepdims=True)
    y = (xc * lax.rsqrt(var + LN_EPS)).reshape(shp)
    return (y * g.astype(jnp.float32) + b.astype(jnp.float32)).astype(x.dtype)


def causal_depthwise_conv(x, w):
    k, c = w.shape
    return lax.conv_general_dilated(
        x, w[:, None, :].astype(x.dtype), window_strides=(1,), padding=[(k - 1, 0)],
        dimension_numbers=("NWC", "WIO", "NWC"), feature_group_count=c)


def diff_attention(q, k, v, lam):
    b, s, h, _, d = q.shape
    nb = s // Q_BLOCK
    scale = d ** -0.5
    qb = q.reshape(b, nb, Q_BLOCK, h, 2, d).transpose(1, 0, 2, 3, 4, 5)
    kpos = jnp.arange(s)

    def block(args):
        q_blk, i = args
        qpos = i * Q_BLOCK + jnp.arange(Q_BLOCK)
        logits = jnp.einsum("bqhcd,bkhcd->bhcqk", q_blk, k,
                            preferred_element_type=jnp.float32) * scale
        mask = kpos[None, :] <= qpos[:, None]
        logits = jnp.where(mask, logits, -jnp.inf)
        p = jax.nn.softmax(logits, axis=-1)
        wts = p[:, :, 0] - lam * p[:, :, 1]
        return jnp.einsum("bhqk,bkhe->bqhe", wts.astype(v.dtype), v)

    out = lax.map(block, (qb, jnp.arange(nb)))
    return out.transpose(1, 0, 2, 3, 4).reshape(b, s, h, v.shape[-1])


def hybrid_mixer(h, w_in, q_norm_g, k_norm_g, lam_q1, lam_k1, lam_q2, lam_k2, lam_init,
                 subln_g, cf_dw, cf_dw_b, cf_ln_g, cf_ln_b, sc_w, w_out):
    b, s, _ = h.shape
    z = h @ w_in
    o1 = DA_WIDTH
    o2 = o1 + DA_WIDTH
    o3 = o2 + DA_WIDTH
    o4 = o3 + CF_WIDTH
    o5 = o4 + CF_WIDTH
    o6 = o5 + SC_WIDTH
    o7 = o6 + SC_WIDTH
    q, k, v, cf_a, cf_gate, sc_b, sc_c, sc_x = jnp.split(z, [o1, o2, o3, o4, o5, o6, o7], axis=-1)

    q = rmsnorm(q.reshape(b, s, DA_HEADS, 2, DA_HEAD), q_norm_g)
    k = rmsnorm(k.reshape(b, s, DA_HEADS, 2, DA_HEAD), k_norm_g)
    v = v.reshape(b, s, DA_HEADS, DA_VDIM)
    lam = (jnp.exp(jnp.sum(lam_q1.astype(jnp.float32) * lam_k1.astype(jnp.float32)))
           - jnp.exp(jnp.sum(lam_q2.astype(jnp.float32) * lam_k2.astype(jnp.float32)))
           + lam_init)
    a = diff_attention(q, k, v, lam)
    a = rmsnorm(a, subln_g) * (1.0 - lam_init)
    a = a.reshape(b, s, DA_WIDTH)

    u = cf_a * jax.nn.sigmoid(cf_gate)
    u = causal_depthwise_conv(u, cf_dw) + cf_dw_b
    u = jax.nn.silu(group_layernorm(u, cf_ln_g, cf_ln_b, CF_GROUPS))

    c = sc_b * causal_depthwise_conv(sc_c * sc_x, sc_w)

    return jnp.concatenate([a, u, c], axis=-1) @ w_out


def peer(h, wq, keys, u_tab, v_tab):
    b, s, d = h.shape
    hc = h.reshape((b * s) // PEER_CHUNK, PEER_CHUNK, d)

    def chunk(xc):
        q = (xc @ wq).reshape(PEER_CHUNK, PEER_HEADS, 2, PEER_QDIM // 2)
        sub = jnp.einsum("thcd,hcnd->thcn", q, keys, preferred_element_type=jnp.float32)
        s_top, i_top = lax.top_k(sub, PEER_TOPK)
        cand = s_top[:, :, 0, :, None] + s_top[:, :, 1, None, :]
        cand = cand.reshape(PEER_CHUNK, PEER_HEADS, PEER_TOPK * PEER_TOPK)
        sel_s, sel_i = lax.top_k(cand, PEER_TOPK)
        i1 = jnp.take_along_axis(i_top[:, :, 0], sel_i // PEER_TOPK, axis=-1)
        i2 = jnp.take_along_axis(i_top[:, :, 1], sel_i % PEER_TOPK, axis=-1)
        eid = i1 * N_KEYS + i2
        g = jax.nn.softmax(sel_s, axis=-1)
        ue = u_tab[eid]
        act = jax.nn.gelu(jnp.einsum("thkd,td->thk", ue, xc), approximate=False)
        ve = v_tab[eid]
        return jnp.einsum("thk,thkd->td", (g * act).astype(xc.dtype), ve)

    return lax.map(chunk, hc).reshape(b, s, d)


def setup_inputs(seed: int = 0) -> dict:
    key = jax.random.key(seed)
    ks = jax.random.split(key, 24)
    f32 = jnp.float32
    nrm = lambda k, shape, sc: jax.random.normal(k, shape, f32) * sc
    gain = lambda k, shape: 1.0 + 0.02 * jax.random.normal(k, shape, f32)
    L = DEPTH
    return {
        "x": jax.random.normal(ks[0], (BATCH, SEQ, D_MODEL), f32),
        "norm1_g": gain(ks[1], (L, D_MODEL)),
        "w_in": nrm(ks[2], (L, D_MODEL, IN_COLS), D_MODEL ** -0.5),
        "q_norm_g": gain(ks[3], (L, DA_HEAD)),
        "k_norm_g": gain(ks[4], (L, DA_HEAD)),
        "lam_q1": nrm(ks[5], (L, DA_HEAD), 0.1),
        "lam_k1": nrm(ks[6], (L, DA_HEAD), 0.1),
        "lam_q2": nrm(ks[7], (L, DA_HEAD), 0.1),
        "lam_k2": nrm(ks[8], (L, DA_HEAD), 0.1),
        "subln_g": gain(ks[9], (L, DA_VDIM)),
        "cf_dw": nrm(ks[10], (L, CF_CONV, CF_WIDTH), CF_CONV ** -0.5),
        "cf_dw_b": nrm(ks[11], (L, CF_WIDTH), 0.02),
        "cf_ln_g": gain(ks[12], (L, CF_WIDTH)),
        "cf_ln_b": nrm(ks[13], (L, CF_WIDTH), 0.02),
        "sc_w": nrm(ks[14], (L, SC_CONV, SC_WIDTH), SC_CONV ** -0.5),
        "w_out": nrm(ks[15], (L, MIX_WIDTH, D_MODEL), MIX_WIDTH ** -0.5),
        "norm2_g": gain(ks[16], (L, D_MODEL)),
        "peer_wq": nrm(ks[17], (L, D_MODEL, PEER_HEADS * PEER_QDIM), D_MODEL ** -0.5),
        "peer_keys": nrm(ks[18], (L, PEER_HEADS, 2, N_KEYS, PEER_QDIM // 2), (PEER_QDIM // 2) ** -0.5),
        "peer_u": nrm(ks[19], (L, N_EXPERTS, D_MODEL), D_MODEL ** -0.5),
        "peer_v": nrm(ks[20], (L, N_EXPERTS, D_MODEL), PEER_HEADS ** -0.5),
    }


def reference(x, norm1_g, w_in, q_norm_g, k_norm_g, lam_q1, lam_k1, lam_q2, lam_k2, subln_g,
              cf_dw, cf_dw_b, cf_ln_g, cf_ln_b, sc_w, w_out, norm2_g,
              peer_wq, peer_keys, peer_u, peer_v):
    for l in range(DEPTH):
        lam_init = 0.8 - 0.6 * math.exp(-0.3 * l)
        x = x + hybrid_mixer(rmsnorm(x, norm1_g[l]), w_in[l], q_norm_g[l], k_norm_g[l],
                             lam_q1[l], lam_k1[l], lam_q2[l], lam_k2[l], lam_init, subln_g[l],
                             cf_dw[l], cf_dw_b[l], cf_ln_g[l], cf_ln_b[l], sc_w[l], w_out[l])
        x = x + peer(rmsnorm(x, norm2_g[l]), peer_wq[l], peer_keys[l], peer_u[l], peer_v[l])
    return x
```

```python
import functools
import math

import jax
import jax.numpy as jnp
from jax import lax
from jax.experimental import pallas as pl
from jax.experimental.pallas import tpu as pltpu

F32 = jnp.float32
BF16 = jnp.bfloat16

DEPTH = 2
RMS_EPS = 1e-6
LN_EPS = 1e-5
DA_HEAD = 64
DA_VDIM = 128
DA_HEADS = 4
DA_WIDTH = 512
CF_WIDTH = 256
CF_GROUPS = 4
CF_CONV = 31
SC_WIDTH = 256
SC_CONV = 3
PEER_HEADS = 8
N_KEYS = 128
PEER_TOPK = 16
PEER_HALF = 128

LANES = 128
SUBLANES = 8
VMEM_LIMIT_BYTES = 56 * 1024 * 1024

HALO = 32

NT_DIMS = (((1,), (1,)), ((), ()))


def _dot(a, b):
    return jnp.dot(a, b, preferred_element_type=F32)


def _dot_nt(a, b):
    return lax.dot_general(a, b, NT_DIMS, preferred_element_type=F32)


def _group_sum(v, ones_bd):
    hi = v.astype(BF16)
    lo = (v - hi.astype(F32)).astype(BF16)
    return _dot(hi, ones_bd) + _dot(lo, ones_bd)


def _block_diag_ones(width, group):
    r = jnp.arange(width) // group
    return (r[:, None] == r[None, :]).astype(BF16)


def _in_proj_kernel(x_ref, g1_ref, w_ref, qg_ref, kg_ref, ones_ref, q_ref, k_ref, v_ref, zc_ref):
    x = x_ref[...]
    xn = x * lax.rsqrt(jnp.mean(x * x, axis=-1, keepdims=True) + RMS_EPS) * g1_ref[...]
    z = _dot(xn.astype(BF16), w_ref[...])
    ones_bd = ones_ref[...]

    def qk_norm(t, g):
        ms = _group_sum(t * t, ones_bd) * (1.0 / DA_HEAD)
        return t * lax.rsqrt(ms + RMS_EPS) * g

    o1, o2, o3 = DA_WIDTH, 2 * DA_WIDTH, 3 * DA_WIDTH
    q_ref[...] = (qk_norm(z[:, :o1], qg_ref[...]) * (DA_HEAD ** -0.5)).astype(BF16)
    k_ref[...] = qk_norm(z[:, o1:o2], kg_ref[...]).astype(BF16)
    v_ref[...] = z[:, o2:o3].astype(BF16)
    zc_ref[...] = z[:, o3:]


def _in_proj(x, g1, w_in_bf, qg, kg, *, tm):
    n, d = x.shape
    cols = w_in_bf.shape[1]
    zc_w = cols - 3 * DA_WIDTH
    ones_bd = _block_diag_ones(DA_WIDTH, DA_HEAD)
    full = lambda shape: pl.BlockSpec(shape, lambda i: (0,) * len(shape))
    return pl.pallas_call(
        _in_proj_kernel,
        grid=(n // tm,),
        in_specs=[pl.BlockSpec((tm, d), lambda i: (i, 0)), full((1, d)), full((d, cols)),
                  full((1, DA_WIDTH)), full((1, DA_WIDTH)), full((DA_WIDTH, DA_WIDTH))],
        out_specs=[pl.BlockSpec((tm, DA_WIDTH), lambda i: (i, 0))] * 3
        + [pl.BlockSpec((tm, zc_w), lambda i: (i, 0))],
        out_shape=[jax.ShapeDtypeStruct((n, DA_WIDTH), BF16)] * 3
        + [jax.ShapeDtypeStruct((n, zc_w), F32)],
        compiler_params=pltpu.CompilerParams(dimension_semantics=("arbitrary",),
                                             vmem_limit_bytes=VMEM_LIMIT_BYTES),
        name="in_proj",
    )(x, g1, w_in_bf, qg, kg, ones_bd)


NEG_BIG = -1e30


def _attn_kernel(lam_ref, sg_ref, q_ref, k_ref, v_ref, o_ref, acc_ref, m_ref, l_ref, *, lam_init, tq):
    qi = pl.program_id(2)
    lv = lam_ref[...]
    lam = (jnp.exp(jnp.sum(lv[0:1] * lv[1:2], axis=-1, keepdims=True))
           - jnp.exp(jnp.sum(lv[2:3] * lv[3:4], axis=-1, keepdims=True)) + lam_init)

    q = q_ref[...]
    lane = lax.broadcasted_iota(jnp.int32, q.shape, 1)
    zero = jnp.zeros_like(q)
    qs = (jnp.where(lane < DA_HEAD, q, zero), jnp.where(lane >= DA_HEAD, q, zero))

    acc_ref[...] = jnp.zeros_like(acc_ref)
    m_ref[...] = jnp.full_like(m_ref, NEG_BIG)
    l_ref[...] = jnp.zeros_like(l_ref)

    def step(j, masked):
        start = pl.multiple_of(j * tq, tq)
        kb = k_ref[pl.ds(start, tq), :]
        vb = v_ref[pl.ds(start, tq), :]
        for c in range(2):
            s = _dot_nt(qs[c], kb)
            if masked:
                row = lax.broadcasted_iota(jnp.int32, s.shape, 0)
                col = lax.broadcasted_iota(jnp.int32, s.shape, 1)
                s = jnp.where(col <= row, s, NEG_BIG)
            m_old = m_ref[c]
            m_new = jnp.maximum(m_old, jnp.max(s, axis=-1, keepdims=True))
            alpha = jnp.exp(m_old - m_new)
            p = jnp.exp(s - m_new)
            l_ref[c] = alpha * l_ref[c] + jnp.sum(p, axis=-1, keepdims=True)
            acc_ref[c] = alpha * acc_ref[c] + _dot(p.astype(BF16), vb)
            m_ref[c] = m_new

    def body(j, carry):
        step(j, False)
        return carry

    lax.fori_loop(0, qi, body, 0)
    step(qi, True)

    o = acc_ref[0] / l_ref[0] - lam * (acc_ref[1] / l_ref[1])
    o = o * lax.rsqrt(jnp.mean(o * o, axis=-1, keepdims=True) + RMS_EPS) * sg_ref[...]
    o_ref[...] = (o * (1.0 - lam_init)).astype(BF16)


def _attention(q, k, v, lam_vecs, subln_g, *, lam_init, tq):
    b, s, _ = q.shape
    qspec = pl.BlockSpec((None, tq, DA_VDIM), lambda bi, h, qi: (bi, qi, h))
    kvspec = pl.BlockSpec((None, s, DA_VDIM), lambda bi, h, qi: (bi, 0, h))
    return pl.pallas_call(
        functools.partial(_attn_kernel, lam_init=lam_init, tq=tq),
        grid=(b, DA_HEADS, s // tq),
        in_specs=[pl.BlockSpec((4, DA_HEAD), lambda bi, h, qi: (0, 0)),
                  pl.BlockSpec((1, DA_VDIM), lambda bi, h, qi: (0, 0)),
                  qspec, kvspec, kvspec],
        out_specs=qspec,
        out_shape=jax.ShapeDtypeStruct(q.shape, BF16),
        scratch_shapes=[pltpu.VMEM((2, tq, DA_VDIM), F32), pltpu.VMEM((2, tq, 1), F32),
                        pltpu.VMEM((2, tq, 1), F32)],
        compiler_params=pltpu.CompilerParams(dimension_semantics=("arbitrary",) * 3,
                                             vmem_limit_bytes=VMEM_LIMIT_BYTES),
        name="diff_attention",
    )(lam_vecs, subln_g, q, k, v)


def _out_proj_kernel(a_ref, zc_ref, halo_ref, x_ref, w_ref, dw_ref, dwb_ref, lng_ref, lnb_ref, scw_ref,
                     g2_ref, ones_ref, x1_ref, xn_ref, ubuf, pbuf, *, tm, tiles_per_seq):
    i = pl.program_id(0)
    keep = jnp.where(i % tiles_per_seq == 0, 0.0, 1.0)
    c1, c2, c3, c4 = CF_WIDTH, 2 * CF_WIDTH, 2 * CF_WIDTH + SC_WIDTH, 2 * CF_WIDTH + 2 * SC_WIDTH

    def glu(z):
        return z[:, :c1] * jax.nn.sigmoid(z[:, c1:c2])

    def cx(z):
        return z[:, c3:c4] * z[:, c4:]

    zc = zc_ref[...]
    halo = halo_ref[...]
    ubuf[0:HALO, :] = glu(halo) * keep
    ubuf[HALO:, :] = glu(zc)
    pbuf[0:HALO, :] = cx(halo) * keep
    pbuf[HALO:, :] = cx(zc)

    dw = dw_ref[...]
    u = jnp.zeros((tm, CF_WIDTH), F32) + dwb_ref[...]
    for j in range(CF_CONV):
        u = u + dw[j:j + 1, :] * ubuf[pl.ds(HALO - (CF_CONV - 1) + j, tm), :]
    ones_bd = ones_ref[...]
    gsz = CF_WIDTH // CF_GROUPS
    mu = _group_sum(u, ones_bd) * (1.0 / gsz)
    uc = u - mu
    var = _group_sum(uc * uc, ones_bd) * (1.0 / gsz)
    un = uc * lax.rsqrt(var + LN_EPS) * lng_ref[...] + lnb_ref[...]
    un = un * jax.nn.sigmoid(un)

    scw = scw_ref[...]
    cc = jnp.zeros((tm, SC_WIDTH), F32)
    for j in range(SC_CONV):
        cc = cc + scw[j:j + 1, :] * pbuf[pl.ds(HALO - (SC_CONV - 1) + j, tm), :]
    cc = zc[:, c2:c3] * cc

    y = (_dot(a_ref[...], w_ref[0:DA_WIDTH, :])
         + _dot(un.astype(BF16), w_ref[DA_WIDTH:DA_WIDTH + CF_WIDTH, :])
         + _dot(cc.astype(BF16), w_ref[DA_WIDTH + CF_WIDTH:, :]))
    x1 = x_ref[...] + y
    x1_ref[...] = x1
    xn = x1 * lax.rsqrt(jnp.mean(x1 * x1, axis=-1, keepdims=True) + RMS_EPS) * g2_ref[...]
    xn_ref[...] = xn.astype(BF16)


def _out_proj(a, zc, x, w_out_bf, cf_dw, cf_dw_b, cf_ln_g, cf_ln_b, sc_w, g2, *, tm, seq):
    n, d = x.shape
    zc_w = zc.shape[1]
    hb = tm // HALO
    ones_bd = _block_diag_ones(CF_WIDTH, CF_WIDTH // CF_GROUPS)
    full = lambda shape: pl.BlockSpec(shape, lambda i: (0,) * len(shape))
    row = lambda w: pl.BlockSpec((tm, w), lambda i: (i, 0))
    return pl.pallas_call(
        functools.partial(_out_proj_kernel, tm=tm, tiles_per_seq=seq // tm),
        grid=(n // tm,),
        in_specs=[row(DA_WIDTH), row(zc_w),
                  pl.BlockSpec((HALO, zc_w), lambda i: (jnp.maximum(i * hb - 1, 0), 0)),
                  row(d), full((d, d)), full((CF_CONV, CF_WIDTH)), full((1, CF_WIDTH)),
                  full((1, CF_WIDTH)), full((1, CF_WIDTH)), full((SC_CONV, SC_WIDTH)), full((1, d)),
                  full((CF_WIDTH, CF_WIDTH))],
        out_specs=[row(d), row(d)],
        out_shape=[jax.ShapeDtypeStruct((n, d), F32), jax.ShapeDtypeStruct((n, d), BF16)],
        scratch_shapes=[pltpu.VMEM((tm + HALO, CF_WIDTH), F32), pltpu.VMEM((tm + HALO, SC_WIDTH), F32)],
        compiler_params=pltpu.CompilerParams(dimension_semantics=("arbitrary",),
                                             vmem_limit_bytes=VMEM_LIMIT_BYTES),
        name="out_proj",
    )(a, zc, zc, x, w_out_bf, cf_dw, cf_dw_b, cf_ln_g, cf_ln_b, sc_w, g2, ones_bd)


def _oddeven_merge_sort_pairs(n):
    pairs = []
    p = 1
    while p < n:
        k = p
        while k >= 1:
            for j in range(k % p, n - k, 2 * k):
                for i in range(min(k, n - j - k)):
                    if (i + j) // (2 * p) == (i + j + k) // (2 * p):
                        pairs.append((i + j, i + j + k))
            k //= 2
        p *= 2
    return pairs


def _bitonic_sort_pairs(n):
    pairs = []
    k = n // 2
    while k >= 1:
        for i in range(n):
            if (i // k) % 2 == 0:
                pairs.append((i, i + k))
        k //= 2
    return pairs


SORT16_PAIRS = _oddeven_merge_sort_pairs(PEER_TOPK)
BITONIC16_PAIRS = _bitonic_sort_pairs(PEER_TOPK)
CAND_CELLS = [(j, k) for j in range(PEER_TOPK) for k in range(PEER_TOPK) if (j + 1) * (k + 1) <= PEER_TOPK]


def _beats(a, b):
    return (a[0] > b[0]) | ((a[0] == b[0]) & (a[1] < b[1]))


def _cmp_exchange(items, i, j):
    a, b = items[i], items[j]
    sw = _beats(b, a)
    items[i] = (jnp.where(sw, b[0], a[0]), jnp.where(sw, b[1], a[1]))
    items[j] = (jnp.where(sw, a[0], b[0]), jnp.where(sw, a[1], b[1]))


def _merge_top(a, b):
    out = []
    for i in range(PEER_TOPK):
        x, y = a[i], b[PEER_TOPK - 1 - i]
        sw = _beats(y, x)
        out.append((jnp.where(sw, y[0], x[0]), jnp.where(sw, y[1], x[1])))
    for (i, j) in BITONIC16_PAIRS:
        _cmp_exchange(out, i, j)
    return out


def _peer_route_kernel(xn_ref, wq_ref, keys_ref, r2_ref, e2_ref, n1_ref, e1_ref,
                       sbuf, sval, sidx, obuf, *, tr):
    ng = tr // LANES
    assert ng == SUBLANES
    q = _dot(xn_ref[...], wq_ref[...]).astype(BF16)
    keys = keys_ref[...].astype(BF16)
    for c in range(2):
        st = _dot_nt(keys[c], q[:, c * PEER_HALF:(c + 1) * PEER_HALF])
        for g in range(ng):
            sbuf[c, g * N_KEYS:(g + 1) * N_KEYS, :] = st[:, g * LANES:(g + 1) * LANES]

    def key_row(c, n):
        return sbuf[c, pl.ds(n, ng, stride=N_KEYS), :]

    n_groups = N_KEYS // PEER_TOPK

    def sort_group(t, carry):
        c = t // n_groups
        grp = t % n_groups
        base = grp * PEER_TOPK
        items = []
        for r in range(PEER_TOPK):
            val = key_row(c, base + r)
            idx = jnp.full(val.shape, r, jnp.int32) + base
            items.append((val, idx))
        for (i, j) in SORT16_PAIRS:
            _cmp_exchange(items, i, j)
        for r in range(PEER_TOPK):
            sval[t, r] = items[r][0]
            sidx[t, r] = items[r][1]
        return carry

    lax.fori_loop(0, 2 * n_groups, sort_group, 0)

    tops = []
    for c in range(2):
        lists = [[(sval[c * n_groups + g, r], sidx[c * n_groups + g, r]) for r in range(PEER_TOPK)]
                 for g in range(n_groups)]
        while len(lists) > 1:
            lists = [_merge_top(lists[2 * m], lists[2 * m + 1]) for m in range(len(lists) // 2)]
        tops.append(lists[0])
    (v1, i1), (v2, i2) = [([p[0] for p in t], [p[1] for p in t]) for t in tops]

    cand = {cell: v1[cell[0]] + v2[cell[1]] for cell in CAND_CELLS}
    rank = {cell: jnp.zeros(v1[0].shape, F32) for cell in CAND_CELLS}
    for pi, p in enumerate(CAND_CELLS):
        for qcell in CAND_CELLS[pi + 1:]:
            if p[0] <= qcell[0] and p[1] <= qcell[1]:
                rank[qcell] = rank[qcell] + 1.0
            else:
                ge = jnp.where(cand[p] >= cand[qcell], 1.0, 0.0)
                rank[qcell] = rank[qcell] + ge
                rank[p] = rank[p] + (1.0 - ge)
    sel = {cell: jnp.where(rank[cell] < PEER_TOPK, 1.0, 0.0) for cell in CAND_CELLS}
    cnt = [sum(sel[(j, k)] for k in range(PEER_TOPK) if (j, k) in sel) for j in range(PEER_TOPK)]

    ej = [jnp.exp(v1[j] - v1[0]) for j in range(PEER_TOPK)]
    ek = [jnp.exp(v2[k] - v2[0]) for k in range(PEER_TOPK)]
    zsum = sum(sel[(j, k)] * (ej[j] * ek[k]) for (j, k) in CAND_CELLS)
    zinv = 1.0 / zsum

    def emit(out_ref, rows):
        for n in range(N_KEYS):
            obuf[pl.ds(n, ng, stride=N_KEYS), :] = rows(n)
        for g in range(ng):
            out_ref[:, g * LANES:(g + 1) * LANES] = obuf[g * N_KEYS:(g + 1) * N_KEYS, :].astype(out_ref.dtype)

    def rank2_row(n):
        acc = jnp.full(v1[0].shape, float(PEER_TOPK), F32)
        for k in range(PEER_TOPK):
            acc = jnp.where(i2[k] == n, float(k), acc)
        return acc

    def count1_row(n):
        acc = jnp.zeros(v1[0].shape, F32)
        for j in range(PEER_TOPK):
            acc = jnp.where(i1[j] == n, cnt[j], acc)
        return acc

    emit(r2_ref, rank2_row)
    emit(n1_ref, count1_row)
    emit(e2_ref, lambda n: jnp.exp(key_row(1, n) - v2[0]))
    emit(e1_ref, lambda n: jnp.exp(key_row(0, n) - v1[0]) * zinv)


def _peer_route(xn, wq_bf, keys, *, tr):
    n, d = xn.shape
    hq = wq_bf.shape[1] // PEER_HEADS
    ng = tr // LANES
    mapspec = pl.BlockSpec((None, N_KEYS, tr), lambda i, h: (h, 0, i))
    mapshapes = [jax.ShapeDtypeStruct((PEER_HEADS, N_KEYS, n), dt) for dt in (BF16, BF16, F32, F32)]
    n_lists = 2 * (N_KEYS // PEER_TOPK)
    return pl.pallas_call(
        functools.partial(_peer_route_kernel, tr=tr),
        grid=(n // tr, PEER_HEADS),
        in_specs=[pl.BlockSpec((tr, d), lambda i, h: (i, 0)),
                  pl.BlockSpec((d, hq), lambda i, h: (0, h)),
                  pl.BlockSpec((None, 2, N_KEYS, PEER_HALF), lambda i, h: (h, 0, 0, 0))],
        out_specs=[mapspec] * 4,
        out_shape=mapshapes,
        scratch_shapes=[pltpu.VMEM((2, ng * N_KEYS, LANES), F32),
                        pltpu.VMEM((n_lists, PEER_TOPK, ng, LANES), F32),
                        pltpu.VMEM((n_lists, PEER_TOPK, ng, LANES), jnp.int32),
                        pltpu.VMEM((ng * N_KEYS, LANES), F32)],
        compiler_params=pltpu.CompilerParams(dimension_semantics=("arbitrary",) * 2,
                                             vmem_limit_bytes=VMEM_LIMIT_BYTES),
        name="peer_route",
    )(xn, wq_bf, keys)


def _gelu(x):
    return 0.5 * x * (1.0 + lax.erf(x * (2.0 ** -0.5)))


def _peer_dense_kernel(xn_ref, u_ref, vt_ref, r2_ref, e2_ref, n1_ref, e1_ref, x1_ref, o_ref,
                       act_ref, w_ref, acc_ref, *, et):
    e = pl.program_id(1)
    ne = pl.num_programs(1)
    a_per = et // N_KEYS

    @pl.when(e == 0)
    def _():
        acc_ref[...] = jnp.zeros_like(acc_ref)

    act_ref[...] = _dot_nt(u_ref[...], xn_ref[...])

    arows = pl.ds(pl.multiple_of(e * a_per, a_per), a_per)
    for al in range(a_per):
        gate = jnp.zeros((N_KEYS, act_ref.shape[1]), BF16)
        for h in range(PEER_HEADS):
            n1 = n1_ref[h, arows, :][al:al + 1].astype(BF16)
            e1 = e1_ref[h, arows, :][al:al + 1].astype(BF16)
            picked = jnp.where(r2_ref[h] < n1, e2_ref[h], jnp.zeros((), BF16))
            gate = gate + picked * e1
        rows = pl.ds(al * N_KEYS, N_KEYS)
        w_ref[rows, :] = _gelu(act_ref[rows, :]).astype(BF16) * gate

    acc_ref[...] += _dot(vt_ref[...], w_ref[...])

    @pl.when(e == ne - 1)
    def _():
        o_ref[...] = x1_ref[...] + acc_ref[...].T


def _peer_dense(xn, u_bf, vt_bf, maps, x1, *, tt, et):
    n, d = xn.shape
    n_exp = u_bf.shape[0]
    mapspec = pl.BlockSpec((PEER_HEADS, N_KEYS, tt), lambda i, e: (0, 0, i))
    return pl.pallas_call(
        functools.partial(_peer_dense_kernel, et=et),
        grid=(n // tt, n_exp // et),
        in_specs=[pl.BlockSpec((tt, d), lambda i, e: (i, 0)),
                  pl.BlockSpec((et, d), lambda i, e: (e, 0)),
                  pl.BlockSpec((d, et), lambda i, e: (0, e)),
                  mapspec, mapspec, mapspec, mapspec,
                  pl.BlockSpec((tt, d), lambda i, e: (i, 0))],
        out_specs=pl.BlockSpec((tt, d), lambda i, e: (i, 0)),
        out_shape=jax.ShapeDtypeStruct((n, d), F32),
        scratch_shapes=[pltpu.VMEM((et, tt), F32), pltpu.VMEM((et, tt), BF16), pltpu.VMEM((d, tt), F32)],
        compiler_params=pltpu.CompilerParams(dimension_semantics=("arbitrary",) * 2,
                                             vmem_limit_bytes=VMEM_LIMIT_BYTES),
        name="peer_dense",
    )(xn, u_bf, vt_bf, *maps, x1)


def _tiles(n_tokens, seq):
    tm = min(512, seq)
    tq = min(256, seq)
    tr = SUBLANES * LANES
    tt = min(512, n_tokens)
    et = 1024
    return tm, tq, tr, tt, et


def _layer(x2d, batch, seq, lam_init, p):
    n, d = x2d.shape
    tm, tq, tr, tt, et = _tiles(n, seq)
    row = lambda v: v.reshape(1, -1)
    q, k, v, zc = _in_proj(x2d, row(p["norm1_g"]), p["w_in"].astype(BF16),
                           row(jnp.tile(p["q_norm_g"], DA_WIDTH // DA_HEAD)),
                           row(jnp.tile(p["k_norm_g"], DA_WIDTH // DA_HEAD)), tm=tm)
    lam_vecs = jnp.stack([p["lam_q1"], p["lam_k1"], p["lam_q2"], p["lam_k2"]])
    shp = (batch, seq, DA_WIDTH)
    a = _attention(q.reshape(shp), k.reshape(shp), v.reshape(shp), lam_vecs, row(p["subln_g"]),
                   lam_init=lam_init, tq=tq).reshape(n, DA_WIDTH)
    x1, xn = _out_proj(a, zc, x2d, p["w_out"].astype(BF16), p["cf_dw"], row(p["cf_dw_b"]),
                       row(p["cf_ln_g"]), row(p["cf_ln_b"]), p["sc_w"], row(p["norm2_g"]), tm=tm, seq=seq)
    maps = _peer_route(xn, p["peer_wq"].astype(BF16), p["peer_keys"], tr=tr)
    return _peer_dense(xn, p["peer_u"].astype(BF16), p["peer_v"].astype(BF16).T, maps, x1, tt=tt, et=et)


def kernel(x, norm1_g, w_in, q_norm_g, k_norm_g, lam_q1, lam_k1, lam_q2, lam_k2, subln_g, cf_dw, cf_dw_b,
           cf_ln_g, cf_ln_b, sc_w, w_out, norm2_g, peer_wq, peer_keys, peer_u, peer_v):
    params = dict(norm1_g=norm1_g, w_in=w_in, q_norm_g=q_norm_g, k_norm_g=k_norm_g, lam_q1=lam_q1,
                  lam_k1=lam_k1, lam_q2=lam_q2, lam_k2=lam_k2, subln_g=subln_g, cf_dw=cf_dw,
                  cf_dw_b=cf_dw_b, cf_ln_g=cf_ln_g, cf_ln_b=cf_ln_b, sc_w=sc_w, w_out=w_out,
                  norm2_g=norm2_g, peer_wq=peer_wq, peer_keys=peer_keys, peer_u=peer_u, peer_v=peer_v)
    batch, seq, d = x.shape
    x2d = x.reshape(batch * seq, d)
    for l in range(DEPTH):
        lam_init = 0.8 - 0.6 * math.exp(-0.3 * l)
        x2d = _layer(x2d, batch, seq, lam_init, {name: val[l] for name, val in params.items()})
    return x2d.reshape(batch, seq, d)
```

```python
import functools
import math

import jax
import jax.numpy as jnp
from jax import lax
from jax.experimental import pallas as pl
from jax.experimental.pallas import tpu as pltpu

F32 = jnp.float32
BF16 = jnp.bfloat16

DEPTH = 2
RMS_EPS = 1e-6
LN_EPS = 1e-5
DA_HEAD = 64
DA_VDIM = 128
DA_HEADS = 4
DA_WIDTH = 512
CF_WIDTH = 256
CF_GROUPS = 4
CF_CONV = 31
SC_WIDTH = 256
SC_CONV = 3
PEER_HEADS = 8
N_KEYS = 128
PEER_TOPK = 16
PEER_HALF = 128

LANES = 128
SUBLANES = 8
VMEM_LIMIT_BYTES = 56 * 1024 * 1024

HALO = 32

NT_DIMS = (((1,), (1,)), ((), ()))


def _dot(a, b):
    return jnp.dot(a, b, preferred_element_type=F32)


def _dot_nt(a, b):
    return lax.dot_general(a, b, NT_DIMS, preferred_element_type=F32)


def _group_sum(v, ones_bd):
    hi = v.astype(BF16)
    lo = (v - hi.astype(F32)).astype(BF16)
    return _dot(hi, ones_bd) + _dot(lo, ones_bd)


def _block_diag_ones(width, group):
    r = jnp.arange(width) // group
    return (r[:, None] == r[None, :]).astype(BF16)


def _in_proj_kernel(x_ref, g1_ref, w_ref, qg_ref, kg_ref, ones_ref, q_ref, k_ref, vt_ref, zc_ref):
    x = x_ref[...]
    xn = x * lax.rsqrt(jnp.mean(x * x, axis=-1, keepdims=True) + RMS_EPS) * g1_ref[...]
    z = _dot(xn.astype(BF16), w_ref[...])
    ones_bd = ones_ref[...]

    def qk_norm(t, g):
        ms = _group_sum(t * t, ones_bd) * (1.0 / DA_HEAD)
        return t * lax.rsqrt(ms + RMS_EPS) * g

    o1, o2, o3 = DA_WIDTH, 2 * DA_WIDTH, 3 * DA_WIDTH
    q_ref[...] = (qk_norm(z[:, :o1], qg_ref[...]) * (DA_HEAD ** -0.5)).astype(BF16)
    k_ref[...] = qk_norm(z[:, o1:o2], kg_ref[...]).astype(BF16)
    vt_ref[...] = z[:, o2:o3].T.astype(BF16)
    zc_ref[...] = z[:, o3:]


def _in_proj(x, g1, w_in_bf, qg, kg, *, tm):
    n, d = x.shape
    cols = w_in_bf.shape[1]
    zc_w = cols - 3 * DA_WIDTH
    ones_bd = _block_diag_ones(DA_WIDTH, DA_HEAD)
    full = lambda shape: pl.BlockSpec(shape, lambda i: (0,) * len(shape))
    return pl.pallas_call(
        _in_proj_kernel,
        grid=(n // tm,),
        in_specs=[pl.BlockSpec((tm, d), lambda i: (i, 0)), full((1, d)), full((d, cols)),
                  full((1, DA_WIDTH)), full((1, DA_WIDTH)), full((DA_WIDTH, DA_WIDTH))],
        out_specs=[pl.BlockSpec((tm, DA_WIDTH), lambda i: (i, 0))] * 2
        + [pl.BlockSpec((DA_WIDTH, tm), lambda i: (0, i)), pl.BlockSpec((tm, zc_w), lambda i: (i, 0))],
        out_shape=[jax.ShapeDtypeStruct((n, DA_WIDTH), BF16)] * 2
        + [jax.ShapeDtypeStruct((DA_WIDTH, n), BF16), jax.ShapeDtypeStruct((n, zc_w), F32)],
        compiler_params=pltpu.CompilerParams(dimension_semantics=("arbitrary",),
                                             vmem_limit_bytes=VMEM_LIMIT_BYTES),
        name="in_proj",
    )(x, g1, w_in_bf, qg, kg, ones_bd)


NEG_BIG = -1e30


def _attn_kernel(lam_ref, sg_ref, q_ref, k_ref, vt_ref, o_ref, acc_ref, m_ref, l_ref, *, lam_init, tq):
    qi = pl.program_id(2)
    lv = lam_ref[...]
    lam = (jnp.exp(jnp.sum(lv[0:1] * lv[1:2], axis=-1, keepdims=True))
           - jnp.exp(jnp.sum(lv[2:3] * lv[3:4], axis=-1, keepdims=True)) + lam_init)

    q = q_ref[...]
    lane = lax.broadcasted_iota(jnp.int32, q.shape, 1)
    zero = jnp.zeros_like(q)
    qs = (jnp.where(lane < DA_HEAD, q, zero), jnp.where(lane >= DA_HEAD, q, zero))

    acc_ref[...] = jnp.zeros_like(acc_ref)
    m_ref[...] = jnp.full_like(m_ref, NEG_BIG)
    l_ref[...] = jnp.zeros_like(l_ref)

    def step(j, masked):
        start = pl.multiple_of(j * tq, tq)
        kb = k_ref[pl.ds(start, tq), :]
        vtb = vt_ref[:, pl.ds(start, tq)]
        for c in range(2):
            s = _dot_nt(kb, qs[c])
            if masked:
                key = lax.broadcasted_iota(jnp.int32, s.shape, 0)
                qry = lax.broadcasted_iota(jnp.int32, s.shape, 1)
                s = jnp.where(key <= qry, s, NEG_BIG)
            m_old = m_ref[c]
            m_new = jnp.maximum(m_old, jnp.max(s, axis=0, keepdims=True))
            alpha = jnp.exp(m_old - m_new)
            p = jnp.exp(s - m_new)
            l_ref[c] = alpha * l_ref[c] + jnp.sum(p, axis=0, keepdims=True)
            acc_ref[c] = alpha * acc_ref[c] + _dot(vtb, p.astype(BF16))
            m_ref[c] = m_new

    def body(j, carry):
        step(j, False)
        return carry

    lax.fori_loop(0, qi, body, 0)
    step(qi, True)

    o = acc_ref[0] / l_ref[0] - lam * (acc_ref[1] / l_ref[1])
    o = o * lax.rsqrt(jnp.mean(o * o, axis=0, keepdims=True) + RMS_EPS) * sg_ref[...]
    o_ref[...] = (o * (1.0 - lam_init)).T.astype(BF16)


def _attention(q, k, vt, lam_vecs, subln_g, *, lam_init, tq):
    b, s, _ = q.shape
    qspec = pl.BlockSpec((None, tq, DA_VDIM), lambda bi, h, qi: (bi, qi, h))
    return pl.pallas_call(
        functools.partial(_attn_kernel, lam_init=lam_init, tq=tq),
        grid=(b, DA_HEADS, s // tq),
        in_specs=[pl.BlockSpec((4, DA_HEAD), lambda bi, h, qi: (0, 0)),
                  pl.BlockSpec((DA_VDIM, 1), lambda bi, h, qi: (0, 0)),
                  qspec,
                  pl.BlockSpec((None, s, DA_VDIM), lambda bi, h, qi: (bi, 0, h)),
                  pl.BlockSpec((DA_VDIM, s), lambda bi, h, qi: (h, bi))],
        out_specs=qspec,
        out_shape=jax.ShapeDtypeStruct(q.shape, BF16),
        scratch_shapes=[pltpu.VMEM((2, DA_VDIM, tq), F32), pltpu.VMEM((2, 1, tq), F32),
                        pltpu.VMEM((2, 1, tq), F32)],
        compiler_params=pltpu.CompilerParams(dimension_semantics=("arbitrary",) * 3,
                                             vmem_limit_bytes=VMEM_LIMIT_BYTES),
        name="diff_attention",
    )(lam_vecs, subln_g, q, k, vt)


def _out_proj_kernel(a_ref, zc_ref, halo_ref, x_ref, w_ref, dw_ref, dwb_ref, lng_ref, lnb_ref, scw_ref,
                     g2_ref, ones_ref, x1_ref, xn_ref, xnt_ref, ubuf, pbuf, *, tm, tiles_per_seq):
    i = pl.program_id(0)
    keep = jnp.where(i % tiles_per_seq == 0, 0.0, 1.0)
    c1, c2, c3, c4 = CF_WIDTH, 2 * CF_WIDTH, 2 * CF_WIDTH + SC_WIDTH, 2 * CF_WIDTH + 2 * SC_WIDTH

    def glu(z):
        return z[:, :c1] * jax.nn.sigmoid(z[:, c1:c2])

    def cx(z):
        return z[:, c3:c4] * z[:, c4:]

    zc = zc_ref[...]
    halo = halo_ref[...]
    ubuf[0:HALO, :] = glu(halo) * keep
    ubuf[HALO:, :] = glu(zc)
    pbuf[0:HALO, :] = cx(halo) * keep
    pbuf[HALO:, :] = cx(zc)

    dw = dw_ref[...]
    u = jnp.zeros((tm, CF_WIDTH), F32) + dwb_ref[...]
    for j in range(CF_CONV):
        u = u + dw[j:j + 1, :] * ubuf[pl.ds(HALO - (CF_CONV - 1) + j, tm), :]
    ones_bd = ones_ref[...]
    gsz = CF_WIDTH // CF_GROUPS
    mu = _group_sum(u, ones_bd) * (1.0 / gsz)
    uc = u - mu
    var = _group_sum(uc * uc, ones_bd) * (1.0 / gsz)
    un = uc * lax.rsqrt(var + LN_EPS) * lng_ref[...] + lnb_ref[...]
    un = un * jax.nn.sigmoid(un)

    scw = scw_ref[...]
    cc = jnp.zeros((tm, SC_WIDTH), F32)
    for j in range(SC_CONV):
        cc = cc + scw[j:j + 1, :] * pbuf[pl.ds(HALO - (SC_CONV - 1) + j, tm), :]
    cc = zc[:, c2:c3] * cc

    y = (_dot(a_ref[...], w_ref[0:DA_WIDTH, :])
         + _dot(un.astype(BF16), w_ref[DA_WIDTH:DA_WIDTH + CF_WIDTH, :])
         + _dot(cc.astype(BF16), w_ref[DA_WIDTH + CF_WIDTH:, :]))
    x1 = x_ref[...] + y
    x1_ref[...] = x1
    xn = x1 * lax.rsqrt(jnp.mean(x1 * x1, axis=-1, keepdims=True) + RMS_EPS) * g2_ref[...]
    xn_ref[...] = xn.astype(BF16)
    xnt_ref[...] = xn.T.astype(BF16)


def _out_proj(a, zc, x, w_out_bf, cf_dw, cf_dw_b, cf_ln_g, cf_ln_b, sc_w, g2, *, tm, seq):
    n, d = x.shape
    zc_w = zc.shape[1]
    hb = tm // HALO
    ones_bd = _block_diag_ones(CF_WIDTH, CF_WIDTH // CF_GROUPS)
    full = lambda shape: pl.BlockSpec(shape, lambda i: (0,) * len(shape))
    row = lambda w: pl.BlockSpec((tm, w), lambda i: (i, 0))
    return pl.pallas_call(
        functools.partial(_out_proj_kernel, tm=tm, tiles_per_seq=seq // tm),
        grid=(n // tm,),
        in_specs=[row(DA_WIDTH), row(zc_w),
                  pl.BlockSpec((HALO, zc_w), lambda i: (jnp.maximum(i * hb - 1, 0), 0)),
                  row(d), full((d, d)), full((CF_CONV, CF_WIDTH)), full((1, CF_WIDTH)),
                  full((1, CF_WIDTH)), full((1, CF_WIDTH)), full((SC_CONV, SC_WIDTH)), full((1, d)),
                  full((CF_WIDTH, CF_WIDTH))],
        out_specs=[row(d), row(d), pl.BlockSpec((d, tm), lambda i: (0, i))],
        out_shape=[jax.ShapeDtypeStruct((n, d), F32), jax.ShapeDtypeStruct((n, d), BF16),
                   jax.ShapeDtypeStruct((d, n), BF16)],
        scratch_shapes=[pltpu.VMEM((tm + HALO, CF_WIDTH), F32), pltpu.VMEM((tm + HALO, SC_WIDTH), F32)],
        compiler_params=pltpu.CompilerParams(dimension_semantics=("arbitrary",),
                                             vmem_limit_bytes=VMEM_LIMIT_BYTES),
        name="out_proj",
    )(a, zc, zc, x, w_out_bf, cf_dw, cf_dw_b, cf_ln_g, cf_ln_b, sc_w, g2, ones_bd)


def _oddeven_merge_sort_pairs(n):
    pairs = []
    p = 1
    while p < n:
        k = p
        while k >= 1:
            for j in range(k % p, n - k, 2 * k):
                for i in range(min(k, n - j - k)):
                    if (i + j) // (2 * p) == (i + j + k) // (2 * p):
                        pairs.append((i + j, i + j + k))
            k //= 2
        p *= 2
    return pairs


def _bitonic_sort_pairs(n):
    pairs = []
    k = n // 2
    while k >= 1:
        for i in range(n):
            if (i // k) % 2 == 0:
                pairs.append((i, i + k))
        k //= 2
    return pairs


SORT16_PAIRS = _oddeven_merge_sort_pairs(PEER_TOPK)
BITONIC16_PAIRS = _bitonic_sort_pairs(PEER_TOPK)
CAND_CELLS = [(j, k) for j in range(PEER_TOPK) for k in range(PEER_TOPK) if (j + 1) * (k + 1) <= PEER_TOPK]


def _beats(a, b):
    return (a[0] > b[0]) | ((a[0] == b[0]) & (a[1] < b[1]))


def _cmp_exchange(items, i, j):
    a, b = items[i], items[j]
    sw = _beats(b, a)
    items[i] = (jnp.where(sw, b[0], a[0]), jnp.where(sw, b[1], a[1]))
    items[j] = (jnp.where(sw, a[0], b[0]), jnp.where(sw, a[1], b[1]))


def _merge_top(a, b):
    out = []
    for i in range(PEER_TOPK):
        x, y = a[i], b[PEER_TOPK - 1 - i]
        sw = _beats(y, x)
        out.append((jnp.where(sw, y[0], x[0]), jnp.where(sw, y[1], x[1])))
    for (i, j) in BITONIC16_PAIRS:
        _cmp_exchange(out, i, j)
    return out


def _peer_route_kernel(xn_ref, wq_ref, keys_ref, r2_ref, e2_ref, n1_ref, e1_ref,
                       sbuf, sval, sidx, obuf, *, tr):
    ng = tr // LANES
    assert ng == SUBLANES
    q = _dot(xn_ref[...], wq_ref[...]).astype(BF16)
    keys = keys_ref[...].astype(BF16)
    for c in range(2):
        st = _dot_nt(keys[c], q[:, c * PEER_HALF:(c + 1) * PEER_HALF])
        for g in range(ng):
            sbuf[c, g * N_KEYS:(g + 1) * N_KEYS, :] = st[:, g * LANES:(g + 1) * LANES]

    def key_row(c, n):
        return sbuf[c, pl.ds(n, ng, stride=N_KEYS), :]

    n_groups = N_KEYS // PEER_TOPK

    def sort_group(t, carry):
        c = t // n_groups
        grp = t % n_groups
        base = grp * PEER_TOPK
        items = []
        for r in range(PEER_TOPK):
            val = key_row(c, base + r)
            idx = jnp.full(val.shape, r, jnp.int32) + base
            items.append((val, idx))
        for (i, j) in SORT16_PAIRS:
            _cmp_exchange(items, i, j)
        for r in range(PEER_TOPK):
            sval[t, r] = items[r][0]
            sidx[t, r] = items[r][1]
        return carry

    lax.fori_loop(0, 2 * n_groups, sort_group, 0)

    tops = []
    for c in range(2):
        lists = [[(sval[c * n_groups + g, r], sidx[c * n_groups + g, r]) for r in range(PEER_TOPK)]
                 for g in range(n_groups)]
        while len(lists) > 1:
            lists = [_merge_top(lists[2 * m], lists[2 * m + 1]) for m in range(len(lists) // 2)]
        tops.append(lists[0])
    (v1, i1), (v2, i2) = [([p[0] for p in t], [p[1] for p in t]) for t in tops]

    cand = {cell: v1[cell[0]] + v2[cell[1]] for cell in CAND_CELLS}
    rank = {cell: jnp.zeros(v1[0].shape, F32) for cell in CAND_CELLS}
    for pi, p in enumerate(CAND_CELLS):
        for qcell in CAND_CELLS[pi + 1:]:
            if p[0] <= qcell[0] and p[1] <= qcell[1]:
                rank[qcell] = rank[qcell] + 1.0
            else:
                ge = jnp.where(cand[p] >= cand[qcell], 1.0, 0.0)
                rank[qcell] = rank[qcell] + ge
                rank[p] = rank[p] + (1.0 - ge)
    sel = {cell: jnp.where(rank[cell] < PEER_TOPK, 1.0, 0.0) for cell in CAND_CELLS}
    cnt = [sum(sel[(j, k)] for k in range(PEER_TOPK) if (j, k) in sel) for j in range(PEER_TOPK)]

    ej = [jnp.exp(v1[j] - v1[0]) for j in range(PEER_TOPK)]
    ek = [jnp.exp(v2[k] - v2[0]) for k in range(PEER_TOPK)]
    zsum = sum(sel[(j, k)] * (ej[j] * ek[k]) for (j, k) in CAND_CELLS)
    zinv = 1.0 / zsum

    def emit(out_ref, rows):
        for n in range(N_KEYS):
            obuf[pl.ds(n, ng, stride=N_KEYS), :] = rows(n)
        for g in range(ng):
            out_ref[:, g * LANES:(g + 1) * LANES] = obuf[g * N_KEYS:(g + 1) * N_KEYS, :].astype(out_ref.dtype)

    def rank2_row(n):
        acc = jnp.full(v1[0].shape, float(PEER_TOPK), F32)
        for k in range(PEER_TOPK):
            acc = jnp.where(i2[k] == n, float(k), acc)
        return acc

    def count1_row(n):
        acc = jnp.zeros(v1[0].shape, F32)
        for j in range(PEER_TOPK):
            acc = jnp.where(i1[j] == n, cnt[j], acc)
        return acc

    emit(r2_ref, rank2_row)
    emit(n1_ref, count1_row)
    emit(e2_ref, lambda n: jnp.exp(key_row(1, n) - v2[0]))
    emit(e1_ref, lambda n: jnp.exp(key_row(0, n) - v1[0]) * zinv)


def _peer_route(xn, wq_bf, keys, *, tr):
    n, d = xn.shape
    hq = wq_bf.shape[1] // PEER_HEADS
    ng = tr // LANES
    mapspec = pl.BlockSpec((None, N_KEYS, tr), lambda i, h: (h, 0, i))
    mapshapes = [jax.ShapeDtypeStruct((PEER_HEADS, N_KEYS, n), dt) for dt in (BF16, BF16, F32, F32)]
    n_lists = 2 * (N_KEYS // PEER_TOPK)
    return pl.pallas_call(
        functools.partial(_peer_route_kernel, tr=tr),
        grid=(n // tr, PEER_HEADS),
        in_specs=[pl.BlockSpec((tr, d), lambda i, h: (i, 0)),
                  pl.BlockSpec((d, hq), lambda i, h: (0, h)),
                  pl.BlockSpec((None, 2, N_KEYS, PEER_HALF), lambda i, h: (h, 0, 0, 0))],
        out_specs=[mapspec] * 4,
        out_shape=mapshapes,
        scratch_shapes=[pltpu.VMEM((2, ng * N_KEYS, LANES), F32),
                        pltpu.VMEM((n_lists, PEER_TOPK, ng, LANES), F32),
                        pltpu.VMEM((n_lists, PEER_TOPK, ng, LANES), jnp.int32),
                        pltpu.VMEM((ng * N_KEYS, LANES), F32)],
        compiler_params=pltpu.CompilerParams(dimension_semantics=("arbitrary",) * 2,
                                             vmem_limit_bytes=VMEM_LIMIT_BYTES),
        name="peer_route",
    )(xn, wq_bf, keys)


def _gelu(x):
    return 0.5 * x * (1.0 + lax.erf(x * (2.0 ** -0.5)))


BF16_ROWS = 2 * SUBLANES


def _peer_dense_kernel(xnt_ref, u_ref, vt_ref, r2_ref, e2_ref, n1_ref, e1_ref, x1_ref, o_ref, acc_ref,
                       *, et, tc):
    e = pl.program_id(1)
    ne = pl.num_programs(1)
    a_per = et // N_KEYS
    tt = xnt_ref.shape[1]
    n_chunks = tt // tc
    packed = (N_KEYS // BF16_ROWS, BF16_ROWS, tc)

    @pl.when(e == 0)
    def _():
        acc_ref[...] = jnp.zeros_like(acc_ref)

    def first_key_row(ref, h, arow, cols):
        row = ref[h, pl.ds(arow, 1), cols]
        return jnp.broadcast_to(row, (BF16_ROWS, tc)).astype(BF16)[None]

    def pre_activation(c):
        return _dot(u_ref[...], xnt_ref[:, c * tc:(c + 1) * tc])

    act_next = pre_activation(0)
    for c in range(n_chunks):
        cols = slice(c * tc, (c + 1) * tc)
        act = act_next
        if c + 1 < n_chunks:
            act_next = pre_activation(c + 1)
        w_rows = []
        for al in range(a_per):
            arow = e * a_per + al
            gate = jnp.zeros(packed, BF16)
            for h in range(PEER_HEADS):
                n1 = first_key_row(n1_ref, h, arow, cols)
                e1 = first_key_row(e1_ref, h, arow, cols)
                r2 = r2_ref[h, :, cols].reshape(packed)
                e2 = e2_ref[h, :, cols].reshape(packed)
                gate = gate + jnp.where(r2 < n1, e2, jnp.zeros((), BF16)) * e1
            g = _gelu(act[al * N_KEYS:(al + 1) * N_KEYS, :]).astype(BF16)
            w_rows.append(g * gate.reshape(N_KEYS, tc))
        w = jnp.concatenate(w_rows, axis=0)
        acc_ref[:, cols] += _dot(vt_ref[...], w)

    @pl.when(e == ne - 1)
    def _():
        o_ref[...] = x1_ref[...] + acc_ref[...].T


def _peer_dense(xnt, u_bf, vt_bf, maps, x1, *, tt, et, tc):
    d, n = xnt.shape
    n_exp = u_bf.shape[0]
    once = pl.Buffered(1)
    mapspec = pl.BlockSpec((PEER_HEADS, N_KEYS, tt), lambda i, e: (0, 0, i), pipeline_mode=once)
    return pl.pallas_call(
        functools.partial(_peer_dense_kernel, et=et, tc=tc),
        grid=(n // tt, n_exp // et),
        in_specs=[pl.BlockSpec((d, tt), lambda i, e: (0, i), pipeline_mode=once),
                  pl.BlockSpec((et, d), lambda i, e: (e, 0)),
                  pl.BlockSpec((d, et), lambda i, e: (0, e)),
                  mapspec, mapspec, mapspec, mapspec,
                  pl.BlockSpec((tt, d), lambda i, e: (i, 0), pipeline_mode=once)],
        out_specs=pl.BlockSpec((tt, d), lambda i, e: (i, 0)),
        out_shape=jax.ShapeDtypeStruct((n, d), F32),
        scratch_shapes=[pltpu.VMEM((d, tt), F32)],
        compiler_params=pltpu.CompilerParams(dimension_semantics=("arbitrary",) * 2,
                                             vmem_limit_bytes=VMEM_LIMIT_BYTES),
        name="peer_dense",
    )(xnt, u_bf, vt_bf, *maps, x1)


def _tiles(n_tokens, seq):
    tm = min(512, seq)
    tq = min(512, seq)
    tr = SUBLANES * LANES
    tt = min(1024, n_tokens)
    et = 1024
    tc = 256
    return tm, tq, tr, tt, et, tc


def _layer(x2d, batch, seq, lam_init, p):
    n, d = x2d.shape
    tm, tq, tr, tt, et, tc = _tiles(n, seq)
    row = lambda v: v.reshape(1, -1)
    q, k, vt, zc = _in_proj(x2d, row(p["norm1_g"]), p["w_in"].astype(BF16),
                            row(jnp.tile(p["q_norm_g"], DA_WIDTH // DA_HEAD)),
                            row(jnp.tile(p["k_norm_g"], DA_WIDTH // DA_HEAD)), tm=tm)
    lam_vecs = jnp.stack([p["lam_q1"], p["lam_k1"], p["lam_q2"], p["lam_k2"]])
    shp = (batch, seq, DA_WIDTH)
    a = _attention(q.reshape(shp), k.reshape(shp), vt, lam_vecs, p["subln_g"].reshape(-1, 1),
                   lam_init=lam_init, tq=tq).reshape(n, DA_WIDTH)
    x1, xn, xnt = _out_proj(a, zc, x2d, p["w_out"].astype(BF16), p["cf_dw"], row(p["cf_dw_b"]),
                            row(p["cf_ln_g"]), row(p["cf_ln_b"]), p["sc_w"], row(p["norm2_g"]), tm=tm, seq=seq)
    maps = _peer_route(xn, p["peer_wq"].astype(BF16), p["peer_keys"], tr=tr)
    return _peer_dense(xnt, p["peer_u"].astype(BF16), p["peer_v"].astype(BF16).T, maps, x1,
                       tt=tt, et=et, tc=tc)


def kernel(x, norm1_g, w_in, q_norm_g, k_norm_g, lam_q1, lam_k1, lam_q2, lam_k2, subln_g, cf_dw, cf_dw_b,
           cf_ln_g, cf_ln_b, sc_w, w_out, norm2_g, peer_wq, peer_keys, peer_u, peer_v):
    params = dict(norm1_g=norm1_g, w_in=w_in, q_norm_g=q_norm_g, k_norm_g=k_norm_g, lam_q1=lam_q1,
                  lam_k1=lam_k1, lam_q2=lam_q2, lam_k2=lam_k2, subln_g=subln_g, cf_dw=cf_dw,
                  cf_dw_b=cf_dw_b, cf_ln_g=cf_ln_g, cf_ln_b=cf_ln_b, sc_w=sc_w, w_out=w_out,
                  norm2_g=norm2_g, peer_wq=peer_wq, peer_keys=peer_keys, peer_u=peer_u, peer_v=peer_v)
    batch, seq, d = x.shape
    x2d = x.reshape(batch * seq, d)
    for l in range(DEPTH):
        lam_init = 0.8 - 0.6 * math.exp(-0.3 * l)
        x2d = _layer(x2d, batch, seq, lam_init, {name: val[l] for name, val in params.items()})
    return x2d.reshape(batch, seq, d)
```

```python
import functools
import math

import jax
import jax.numpy as jnp
from jax import lax
from jax.experimental import pallas as pl
from jax.experimental.pallas import tpu as pltpu

F32 = jnp.float32
BF16 = jnp.bfloat16

DEPTH = 2
RMS_EPS = 1e-6
LN_EPS = 1e-5
LOG2_E = math.log2(math.e)
DA_HEAD = 64
DA_VDIM = 128
DA_HEADS = 4
DA_WIDTH = 512
CF_WIDTH = 256
CF_GROUPS = 4
CF_CONV = 31
SC_WIDTH = 256
SC_CONV = 3
PEER_HEADS = 8
N_KEYS = 128
PEER_TOPK = 16
PEER_HALF = 128

LANES = 128
SUBLANES = 8
VMEM_LIMIT_BYTES = 56 * 1024 * 1024

HALO = 32

NT_DIMS = (((1,), (1,)), ((), ()))


def _dot(a, b):
    return jnp.dot(a, b, preferred_element_type=F32)


def _dot_nt(a, b):
    return lax.dot_general(a, b, NT_DIMS, preferred_element_type=F32)


def _group_sum(v, ones_bd):
    hi = v.astype(BF16)
    lo = (v - hi.astype(F32)).astype(BF16)
    return _dot(hi, ones_bd) + _dot(lo, ones_bd)


def _block_diag_ones(width, group):
    r = jnp.arange(width) // group
    return (r[:, None] == r[None, :]).astype(BF16)


def _in_proj_kernel(x_ref, g1_ref, w_ref, qg_ref, kg_ref, ones_ref, q_ref, k_ref, vt_ref, zc_ref):
    x = x_ref[...]
    xn = x * lax.rsqrt(jnp.mean(x * x, axis=-1, keepdims=True) + RMS_EPS) * g1_ref[...]
    z = _dot(xn.astype(BF16), w_ref[...])
    ones_bd = ones_ref[...]

    def qk_norm(t, g):
        ms = _group_sum(t * t, ones_bd) * (1.0 / DA_HEAD)
        return t * lax.rsqrt(ms + RMS_EPS) * g

    o1, o2, o3 = DA_WIDTH, 2 * DA_WIDTH, 3 * DA_WIDTH
    q_ref[...] = (qk_norm(z[:, :o1], qg_ref[...]) * (DA_HEAD ** -0.5 * LOG2_E)).astype(BF16)
    k_ref[...] = qk_norm(z[:, o1:o2], kg_ref[...]).astype(BF16)
    vt_ref[...] = z[:, o2:o3].T.astype(BF16)
    zc_ref[...] = z[:, o3:]


def _in_proj(x, g1, w_in_bf, qg, kg, *, tm):
    n, d = x.shape
    cols = w_in_bf.shape[1]
    zc_w = cols - 3 * DA_WIDTH
    ones_bd = _block_diag_ones(DA_WIDTH, DA_HEAD)
    full = lambda shape: pl.BlockSpec(shape, lambda i: (0,) * len(shape))
    return pl.pallas_call(
        _in_proj_kernel,
        grid=(n // tm,),
        in_specs=[pl.BlockSpec((tm, d), lambda i: (i, 0)), full((1, d)), full((d, cols)),
                  full((1, DA_WIDTH)), full((1, DA_WIDTH)), full((DA_WIDTH, DA_WIDTH))],
        out_specs=[pl.BlockSpec((tm, DA_WIDTH), lambda i: (i, 0))] * 2
        + [pl.BlockSpec((DA_WIDTH, tm), lambda i: (0, i)), pl.BlockSpec((tm, zc_w), lambda i: (i, 0))],
        out_shape=[jax.ShapeDtypeStruct((n, DA_WIDTH), BF16)] * 2
        + [jax.ShapeDtypeStruct((DA_WIDTH, n), BF16), jax.ShapeDtypeStruct((n, zc_w), F32)],
        compiler_params=pltpu.CompilerParams(dimension_semantics=("arbitrary",),
                                             vmem_limit_bytes=VMEM_LIMIT_BYTES),
        name="in_proj",
    )(x, g1, w_in_bf, qg, kg, ones_bd)


NEG_BIG = -1e30


def _attn_kernel(lam_ref, sg_ref, q_ref, k_ref, vt_ref, o_ref, acc_ref, m_ref, l_ref, *, lam_init, tq):
    qi = pl.program_id(2)
    lv = lam_ref[...]
    lam = (jnp.exp(jnp.sum(lv[0:1] * lv[1:2], axis=-1, keepdims=True))
           - jnp.exp(jnp.sum(lv[2:3] * lv[3:4], axis=-1, keepdims=True)) + lam_init)

    q = q_ref[...]
    lane = lax.broadcasted_iota(jnp.int32, q.shape, 1)
    zero = jnp.zeros_like(q)
    qs = (jnp.where(lane < DA_HEAD, q, zero), jnp.where(lane >= DA_HEAD, q, zero))

    acc_ref[...] = jnp.zeros_like(acc_ref)
    m_ref[...] = jnp.full_like(m_ref, NEG_BIG)
    l_ref[...] = jnp.zeros_like(l_ref)

    def step(j, masked):
        start = pl.multiple_of(j * tq, tq)
        vtb = vt_ref[:, pl.ds(start, tq)]

        def scores(c):
            s = _dot_nt(k_ref[pl.ds(start, tq), :], qs[c])
            if masked:
                key = lax.broadcasted_iota(jnp.int32, s.shape, 0)
                qry = lax.broadcasted_iota(jnp.int32, s.shape, 1)
                s = jnp.where(key <= qry, s, NEG_BIG)
            return s

        for c in range(2):
            m_old = m_ref[c]
            m_new = jnp.maximum(m_old, jnp.max(scores(c), axis=0, keepdims=True))
            m_ref[c] = m_new
            alpha = jnp.exp2(m_old - m_new)
            p = jnp.exp2(scores(c) - m_new)
            l_ref[c] = alpha * l_ref[c] + jnp.sum(p, axis=0, keepdims=True)
            acc_ref[c] = alpha * acc_ref[c] + _dot(vtb, p.astype(BF16))

    def body(j, carry):
        step(j, False)
        return carry

    lax.fori_loop(0, qi, body, 0)
    step(qi, True)

    o = acc_ref[0] / l_ref[0] - lam * (acc_ref[1] / l_ref[1])
    o = o * lax.rsqrt(jnp.mean(o * o, axis=0, keepdims=True) + RMS_EPS) * sg_ref[...]
    o_ref[...] = (o * (1.0 - lam_init)).T.astype(BF16)


def _attention(q, k, vt, lam_vecs, subln_g, *, lam_init, tq):
    b, s, _ = q.shape
    qspec = pl.BlockSpec((None, tq, DA_VDIM), lambda bi, h, qi: (bi, qi, h))
    return pl.pallas_call(
        functools.partial(_attn_kernel, lam_init=lam_init, tq=tq),
        grid=(b, DA_HEADS, s // tq),
        in_specs=[pl.BlockSpec((4, DA_HEAD), lambda bi, h, qi: (0, 0)),
                  pl.BlockSpec((DA_VDIM, 1), lambda bi, h, qi: (0, 0)),
                  qspec,
                  pl.BlockSpec((None, s, DA_VDIM), lambda bi, h, qi: (bi, 0, h)),
                  pl.BlockSpec((DA_VDIM, s), lambda bi, h, qi: (h, bi))],
        out_specs=qspec,
        out_shape=jax.ShapeDtypeStruct(q.shape, BF16),
        scratch_shapes=[pltpu.VMEM((2, DA_VDIM, tq), F32), pltpu.VMEM((2, 1, tq), F32),
                        pltpu.VMEM((2, 1, tq), F32)],
        compiler_params=pltpu.CompilerParams(dimension_semantics=("arbitrary",) * 3,
                                             vmem_limit_bytes=VMEM_LIMIT_BYTES),
        name="diff_attention",
    )(lam_vecs, subln_g, q, k, vt)


def _out_proj_kernel(a_ref, zc_ref, halo_ref, x_ref, w_ref, dw_ref, dwb_ref, lng_ref, lnb_ref, scw_ref,
                     g2_ref, ones_ref, x1_ref, xn_ref, xnt_ref, ubuf, pbuf, *, tm, tiles_per_seq):
    i = pl.program_id(0)
    keep = jnp.where(i % tiles_per_seq == 0, 0.0, 1.0)
    c1, c2, c3, c4 = CF_WIDTH, 2 * CF_WIDTH, 2 * CF_WIDTH + SC_WIDTH, 2 * CF_WIDTH + 2 * SC_WIDTH

    def glu(z):
        return z[:, :c1] * jax.nn.sigmoid(z[:, c1:c2])

    def cx(z):
        return z[:, c3:c4] * z[:, c4:]

    zc = zc_ref[...]
    halo = halo_ref[...]
    ubuf[0:HALO, :] = glu(halo) * keep
    ubuf[HALO:, :] = glu(zc)
    pbuf[0:HALO, :] = cx(halo) * keep
    pbuf[HALO:, :] = cx(zc)

    dw = dw_ref[...]
    u = jnp.zeros((tm, CF_WIDTH), F32) + dwb_ref[...]
    for j in range(CF_CONV):
        u = u + dw[j:j + 1, :] * ubuf[pl.ds(HALO - (CF_CONV - 1) + j, tm), :]
    ones_bd = ones_ref[...]
    gsz = CF_WIDTH // CF_GROUPS
    mu = _group_sum(u, ones_bd) * (1.0 / gsz)
    uc = u - mu
    var = _group_sum(uc * uc, ones_bd) * (1.0 / gsz)
    un = uc * lax.rsqrt(var + LN_EPS) * lng_ref[...] + lnb_ref[...]
    un = un * jax.nn.sigmoid(un)

    scw = scw_ref[...]
    cc = jnp.zeros((tm, SC_WIDTH), F32)
    for j in range(SC_CONV):
        cc = cc + scw[j:j + 1, :] * pbuf[pl.ds(HALO - (SC_CONV - 1) + j, tm), :]
    cc = zc[:, c2:c3] * cc

    y = (_dot(a_ref[...], w_ref[0:DA_WIDTH, :])
         + _dot(un.astype(BF16), w_ref[DA_WIDTH:DA_WIDTH + CF_WIDTH, :])
         + _dot(cc.astype(BF16), w_ref[DA_WIDTH + CF_WIDTH:, :]))
    x1 = x_ref[...] + y
    x1_ref[...] = x1
    xn = x1 * lax.rsqrt(jnp.mean(x1 * x1, axis=-1, keepdims=True) + RMS_EPS) * g2_ref[...]
    xn_ref[...] = xn.astype(BF16)
    xnt_ref[...] = xn.T.astype(BF16)


def _out_proj(a, zc, x, w_out_bf, cf_dw, cf_dw_b, cf_ln_g, cf_ln_b, sc_w, g2, *, tm, seq):
    n, d = x.shape
    zc_w = zc.shape[1]
    hb = tm // HALO
    ones_bd = _block_diag_ones(CF_WIDTH, CF_WIDTH // CF_GROUPS)
    full = lambda shape: pl.BlockSpec(shape, lambda i: (0,) * len(shape))
    row = lambda w: pl.BlockSpec((tm, w), lambda i: (i, 0))
    return pl.pallas_call(
        functools.partial(_out_proj_kernel, tm=tm, tiles_per_seq=seq // tm),
        grid=(n // tm,),
        in_specs=[row(DA_WIDTH), row(zc_w),
                  pl.BlockSpec((HALO, zc_w), lambda i: (jnp.maximum(i * hb - 1, 0), 0)),
                  row(d), full((d, d)), full((CF_CONV, CF_WIDTH)), full((1, CF_WIDTH)),
                  full((1, CF_WIDTH)), full((1, CF_WIDTH)), full((SC_CONV, SC_WIDTH)), full((1, d)),
                  full((CF_WIDTH, CF_WIDTH))],
        out_specs=[row(d), row(d), pl.BlockSpec((d, tm), lambda i: (0, i))],
        out_shape=[jax.ShapeDtypeStruct((n, d), F32), jax.ShapeDtypeStruct((n, d), BF16),
                   jax.ShapeDtypeStruct((d, n), BF16)],
        scratch_shapes=[pltpu.VMEM((tm + HALO, CF_WIDTH), F32), pltpu.VMEM((tm + HALO, SC_WIDTH), F32)],
        compiler_params=pltpu.CompilerParams(dimension_semantics=("arbitrary",),
                                             vmem_limit_bytes=VMEM_LIMIT_BYTES),
        name="out_proj",
    )(a, zc, zc, x, w_out_bf, cf_dw, cf_dw_b, cf_ln_g, cf_ln_b, sc_w, g2, ones_bd)


def _oddeven_merge_sort_pairs(n):
    pairs = []
    p = 1
    while p < n:
        k = p
        while k >= 1:
            for j in range(k % p, n - k, 2 * k):
                for i in range(min(k, n - j - k)):
                    if (i + j) // (2 * p) == (i + j + k) // (2 * p):
                        pairs.append((i + j, i + j + k))
            k //= 2
        p *= 2
    return pairs


def _bitonic_sort_pairs(n):
    pairs = []
    k = n // 2
    while k >= 1:
        for i in range(n):
            if (i // k) % 2 == 0:
                pairs.append((i, i + k))
        k //= 2
    return pairs


SORT16_PAIRS = _oddeven_merge_sort_pairs(PEER_TOPK)
BITONIC16_PAIRS = _bitonic_sort_pairs(PEER_TOPK)
CAND_CELLS = [(j, k) for j in range(PEER_TOPK) for k in range(PEER_TOPK) if (j + 1) * (k + 1) <= PEER_TOPK]


def _beats(a, b):
    return (a[0] > b[0]) | ((a[0] == b[0]) & (a[1] < b[1]))


def _cmp_exchange(items, i, j):
    a, b = items[i], items[j]
    sw = _beats(b, a)
    items[i] = (jnp.where(sw, b[0], a[0]), jnp.where(sw, b[1], a[1]))
    items[j] = (jnp.where(sw, a[0], b[0]), jnp.where(sw, a[1], b[1]))


def _merge_top(a, b):
    out = []
    for i in range(PEER_TOPK):
        x, y = a[i], b[PEER_TOPK - 1 - i]
        sw = _beats(y, x)
        out.append((jnp.where(sw, y[0], x[0]), jnp.where(sw, y[1], x[1])))
    for (i, j) in BITONIC16_PAIRS:
        _cmp_exchange(out, i, j)
    return out


def _peer_route_kernel(xn_ref, wq_ref, keys_ref, r2_ref, e2_ref, n1_ref, e1_ref,
                       sbuf, sval, sidx, obuf, *, tr):
    ng = tr // LANES
    assert ng == SUBLANES
    q = _dot(xn_ref[...], wq_ref[...]).astype(BF16)
    keys = keys_ref[...].astype(BF16)
    for c in range(2):
        st = _dot_nt(keys[c], q[:, c * PEER_HALF:(c + 1) * PEER_HALF])
        for g in range(ng):
            sbuf[c, g * N_KEYS:(g + 1) * N_KEYS, :] = st[:, g * LANES:(g + 1) * LANES]

    def key_row(c, n):
        return sbuf[c, pl.ds(n, ng, stride=N_KEYS), :]

    n_groups = N_KEYS // PEER_TOPK

    def sort_group(t, carry):
        c = t // n_groups
        grp = t % n_groups
        base = grp * PEER_TOPK
        items = []
        for r in range(PEER_TOPK):
            val = key_row(c, base + r)
            idx = jnp.full(val.shape, r, jnp.int32) + base
            items.append((val, idx))
        for (i, j) in SORT16_PAIRS:
            _cmp_exchange(items, i, j)
        for r in range(PEER_TOPK):
            sval[t, r] = items[r][0]
            sidx[t, r] = items[r][1]
        return carry

    lax.fori_loop(0, 2 * n_groups, sort_group, 0)

    tops = []
    for c in range(2):
        lists = [[(sval[c * n_groups + g, r], sidx[c * n_groups + g, r]) for r in range(PEER_TOPK)]
                 for g in range(n_groups)]
        while len(lists) > 1:
            lists = [_merge_top(lists[2 * m], lists[2 * m + 1]) for m in range(len(lists) // 2)]
        tops.append(lists[0])
    (v1, i1), (v2, i2) = [([p[0] for p in t], [p[1] for p in t]) for t in tops]

    cand = {cell: v1[cell[0]] + v2[cell[1]] for cell in CAND_CELLS}
    rank = {cell: jnp.zeros(v1[0].shape, F32) for cell in CAND_CELLS}
    for pi, p in enumerate(CAND_CELLS):
        for qcell in CAND_CELLS[pi + 1:]:
            if p[0] <= qcell[0] and p[1] <= qcell[1]:
                rank[qcell] = rank[qcell] + 1.0
            else:
                ge = jnp.where(cand[p] >= cand[qcell], 1.0, 0.0)
                rank[qcell] = rank[qcell] + ge
                rank[p] = rank[p] + (1.0 - ge)
    sel = {cell: jnp.where(rank[cell] < PEER_TOPK, 1.0, 0.0) for cell in CAND_CELLS}
    cnt = [sum(sel[(j, k)] for k in range(PEER_TOPK) if (j, k) in sel) for j in range(PEER_TOPK)]

    ej = [jnp.exp(v1[j] - v1[0]) for j in range(PEER_TOPK)]
    ek = [jnp.exp(v2[k] - v2[0]) for k in range(PEER_TOPK)]
    zsum = sum(sel[(j, k)] * (ej[j] * ek[k]) for (j, k) in CAND_CELLS)
    zinv = 1.0 / zsum

    def emit(out_ref, rows):
        for n in range(N_KEYS):
            obuf[pl.ds(n, ng, stride=N_KEYS), :] = rows(n)
        for g in range(ng):
            out_ref[:, g * LANES:(g + 1) * LANES] = obuf[g * N_KEYS:(g + 1) * N_KEYS, :].astype(out_ref.dtype)

    def rank2_row(n):
        acc = jnp.full(v1[0].shape, float(PEER_TOPK), F32)
        for k in range(PEER_TOPK):
            acc = jnp.where(i2[k] == n, float(k), acc)
        return acc

    def count1_row(n):
        acc = jnp.zeros(v1[0].shape, F32)
        for j in range(PEER_TOPK):
            acc = jnp.where(i1[j] == n, cnt[j], acc)
        return acc

    emit(r2_ref, rank2_row)
    emit(n1_ref, count1_row)
    emit(e2_ref, lambda n: jnp.exp(key_row(1, n) - v2[0]))
    emit(e1_ref, lambda n: jnp.exp(key_row(0, n) - v1[0]) * zinv)


def _peer_route(xn, wq_bf, keys, *, tr):
    n, d = xn.shape
    hq = wq_bf.shape[1] // PEER_HEADS
    ng = tr // LANES
    mapspec = pl.BlockSpec((None, N_KEYS, tr), lambda i, h: (h, 0, i))
    mapshapes = [jax.ShapeDtypeStruct((PEER_HEADS, N_KEYS, n), dt) for dt in (BF16, BF16, F32, F32)]
    n_lists = 2 * (N_KEYS // PEER_TOPK)
    return pl.pallas_call(
        functools.partial(_peer_route_kernel, tr=tr),
        grid=(n // tr, PEER_HEADS),
        in_specs=[pl.BlockSpec((tr, d), lambda i, h: (i, 0)),
                  pl.BlockSpec((d, hq), lambda i, h: (0, h)),
                  pl.BlockSpec((None, 2, N_KEYS, PEER_HALF), lambda i, h: (h, 0, 0, 0))],
        out_specs=[mapspec] * 4,
        out_shape=mapshapes,
        scratch_shapes=[pltpu.VMEM((2, ng * N_KEYS, LANES), F32),
                        pltpu.VMEM((n_lists, PEER_TOPK, ng, LANES), F32),
                        pltpu.VMEM((n_lists, PEER_TOPK, ng, LANES), jnp.int32),
                        pltpu.VMEM((ng * N_KEYS, LANES), F32)],
        compiler_params=pltpu.CompilerParams(dimension_semantics=("arbitrary",) * 2,
                                             vmem_limit_bytes=VMEM_LIMIT_BYTES),
        name="peer_route",
    )(xn, wq_bf, keys)


def _gelu_x2(x):
    return x * (1.0 + lax.erf(x * (2.0 ** -0.5)))


BF16_ROWS = 2 * SUBLANES


def _peer_dense_kernel(xnt_ref, u_ref, vt_ref, r2_ref, e2_ref, n1_ref, e1_ref, x1_ref, o_ref,
                       acc_ref, act_ref, w_ref, *, et, tc):
    e = pl.program_id(1)
    ne = pl.num_programs(1)
    a_per = et // N_KEYS
    tt = xnt_ref.shape[1]
    n_chunks = tt // tc
    packed = (N_KEYS // BF16_ROWS, BF16_ROWS, tc)

    @pl.when(e == 0)
    def _():
        acc_ref[...] = jnp.zeros_like(acc_ref)

    def first_key_row(ref, h, arow, cols):
        row = ref[h, pl.ds(arow, 1), cols]
        return jnp.broadcast_to(row, (BF16_ROWS, tc)).astype(BF16)[None]

    def pre_activation(c):
        return _dot(u_ref[...], xnt_ref[:, c * tc:(c + 1) * tc])

    act_ref[0] = pre_activation(0)
    for c in range(n_chunks):
        cols = slice(c * tc, (c + 1) * tc)
        if c + 1 < n_chunks:
            act_ref[(c + 1) % 2] = pre_activation(c + 1)
        for al in range(a_per):
            arow = e * a_per + al
            rows = slice(al * N_KEYS, (al + 1) * N_KEYS)
            gate = jnp.zeros(packed, BF16)
            for h in range(PEER_HEADS):
                n1 = first_key_row(n1_ref, h, arow, cols)
                e1 = first_key_row(e1_ref, h, arow, cols)
                r2 = r2_ref[h, :, cols].reshape(packed)
                e2 = e2_ref[h, :, cols].reshape(packed)
                gate = gate + jnp.where(r2 < n1, e2, jnp.zeros((), BF16)) * e1
            g = _gelu_x2(act_ref[c % 2, rows, :]).astype(BF16)
            w_ref[c, rows, :] = g * gate.reshape(N_KEYS, tc)
        acc_ref[:, cols] += _dot(vt_ref[...], w_ref[c])

    @pl.when(e == ne - 1)
    def _():
        o_ref[...] = x1_ref[...] + acc_ref[...].T


def _peer_dense(xnt, u_bf, vt_bf, maps, x1, *, tt, et, tc):
    d, n = xnt.shape
    n_exp = u_bf.shape[0]
    once = pl.Buffered(1)
    mapspec = pl.BlockSpec((PEER_HEADS, N_KEYS, tt), lambda i, e: (0, 0, i), pipeline_mode=once)
    return pl.pallas_call(
        functools.partial(_peer_dense_kernel, et=et, tc=tc),
        grid=(n // tt, n_exp // et),
        in_specs=[pl.BlockSpec((d, tt), lambda i, e: (0, i), pipeline_mode=once),
                  pl.BlockSpec((et, d), lambda i, e: (e, 0)),
                  pl.BlockSpec((d, et), lambda i, e: (0, e)),
                  mapspec, mapspec, mapspec, mapspec,
                  pl.BlockSpec((tt, d), lambda i, e: (i, 0), pipeline_mode=once)],
        out_specs=pl.BlockSpec((tt, d), lambda i, e: (i, 0)),
        out_shape=jax.ShapeDtypeStruct((n, d), F32),
        scratch_shapes=[pltpu.VMEM((d, tt), F32), pltpu.VMEM((2, et, tc), F32),
                        pltpu.VMEM((tt // tc, et, tc), BF16)],
        compiler_params=pltpu.CompilerParams(dimension_semantics=("arbitrary",) * 2,
                                             vmem_limit_bytes=VMEM_LIMIT_BYTES),
        name="peer_dense",
    )(xnt, u_bf, vt_bf, *maps, x1)


def _tiles(n_tokens, seq):
    tm = min(512, seq)
    tq = min(512, seq)
    tr = SUBLANES * LANES
    tt = min(1024, n_tokens)
    et = 1024
    tc = 256
    return tm, tq, tr, tt, et, tc


def _layer(x2d, batch, seq, lam_init, p):
    n, d = x2d.shape
    tm, tq, tr, tt, et, tc = _tiles(n, seq)
    row = lambda v: v.reshape(1, -1)
    q, k, vt, zc = _in_proj(x2d, row(p["norm1_g"]), p["w_in"].astype(BF16),
                            row(jnp.tile(p["q_norm_g"], DA_WIDTH // DA_HEAD)),
                            row(jnp.tile(p["k_norm_g"], DA_WIDTH // DA_HEAD)), tm=tm)
    lam_vecs = jnp.stack([p["lam_q1"], p["lam_k1"], p["lam_q2"], p["lam_k2"]])
    shp = (batch, seq, DA_WIDTH)
    a = _attention(q.reshape(shp), k.reshape(shp), vt, lam_vecs, p["subln_g"].reshape(-1, 1),
                   lam_init=lam_init, tq=tq).reshape(n, DA_WIDTH)
    x1, xn, xnt = _out_proj(a, zc, x2d, p["w_out"].astype(BF16), p["cf_dw"], row(p["cf_dw_b"]),
                            row(p["cf_ln_g"]), row(p["cf_ln_b"]), p["sc_w"], row(p["norm2_g"]), tm=tm, seq=seq)
    maps = _peer_route(xn, p["peer_wq"].astype(BF16), p["peer_keys"], tr=tr)
    half_vt = (0.5 * p["peer_v"]).astype(BF16).T
    return _peer_dense(xnt, p["peer_u"].astype(BF16), half_vt, maps, x1,
                       tt=tt, et=et, tc=tc)


def kernel(x, norm1_g, w_in, q_norm_g, k_norm_g, lam_q1, lam_k1, lam_q2, lam_k2, subln_g, cf_dw, cf_dw_b,
           cf_ln_g, cf_ln_b, sc_w, w_out, norm2_g, peer_wq, peer_keys, peer_u, peer_v):
    params = dict(norm1_g=norm1_g, w_in=w_in, q_norm_g=q_norm_g, k_norm_g=k_norm_g, lam_q1=lam_q1,
                  lam_k1=lam_k1, lam_q2=lam_q2, lam_k2=lam_k2, subln_g=subln_g, cf_dw=cf_dw,
                  cf_dw_b=cf_dw_b, cf_ln_g=cf_ln_g, cf_ln_b=cf_ln_b, sc_w=sc_w, w_out=w_out,
                  norm2_g=norm2_g, peer_wq=peer_wq, peer_keys=peer_keys, peer_u=peer_u, peer_v=peer_v)
    batch, seq, d = x.shape
    x2d = x.reshape(batch * seq, d)
    for l in range(DEPTH):
        lam_init = 0.8 - 0.6 * math.exp(-0.3 * l)
        x2d = _layer(x2d, batch, seq, lam_init, {name: val[l] for name, val in params.items()})
    return x2d.reshape(batch, seq, d)
```

```python
import functools
import math

import jax
import jax.numpy as jnp
from jax import lax
from jax.experimental import pallas as pl
from jax.experimental.pallas import tpu as pltpu

F32 = jnp.float32
BF16 = jnp.bfloat16

DEPTH = 2
RMS_EPS = 1e-6
LN_EPS = 1e-5
LOG2_E = math.log2(math.e)
DA_HEAD = 64
DA_VDIM = 128
DA_HEADS = 4
DA_WIDTH = 512
CF_WIDTH = 256
CF_GROUPS = 4
CF_CONV = 31
SC_WIDTH = 256
SC_CONV = 3
PEER_HEADS = 8
N_KEYS = 128
PEER_TOPK = 16
PEER_HALF = 128

LANES = 128
SUBLANES = 8
VMEM_LIMIT_BYTES = 56 * 1024 * 1024

HALO = 32

NT_DIMS = (((1,), (1,)), ((), ()))


def _dot(a, b):
    return jnp.dot(a, b, preferred_element_type=F32)


def _dot_nt(a, b):
    return lax.dot_general(a, b, NT_DIMS, preferred_element_type=F32)


def _group_sum(v, ones_bd):
    hi = v.astype(BF16)
    lo = (v - hi.astype(F32)).astype(BF16)
    return _dot(hi, ones_bd) + _dot(lo, ones_bd)


def _block_diag_ones(width, group):
    r = jnp.arange(width) // group
    return (r[:, None] == r[None, :]).astype(BF16)


def _in_proj_kernel(x_ref, g1_ref, w_ref, qg_ref, kg_ref, ones_ref, q_ref, k_ref, vt_ref, zc_ref):
    x = x_ref[...]
    xn = x * lax.rsqrt(jnp.mean(x * x, axis=-1, keepdims=True) + RMS_EPS) * g1_ref[...]
    z = _dot(xn.astype(BF16), w_ref[...])
    ones_bd = ones_ref[...]

    def qk_norm(t, g):
        ms = _group_sum(t * t, ones_bd) * (1.0 / DA_HEAD)
        return t * lax.rsqrt(ms + RMS_EPS) * g

    o1, o2, o3 = DA_WIDTH, 2 * DA_WIDTH, 3 * DA_WIDTH
    q_ref[...] = (qk_norm(z[:, :o1], qg_ref[...]) * (DA_HEAD ** -0.5 * LOG2_E)).astype(BF16)
    k_ref[...] = qk_norm(z[:, o1:o2], kg_ref[...]).astype(BF16)
    vt_ref[...] = z[:, o2:o3].T.astype(BF16)
    zc_ref[...] = z[:, o3:]


def _in_proj(x, g1, w_in_bf, qg, kg, *, tm):
    n, d = x.shape
    cols = w_in_bf.shape[1]
    zc_w = cols - 3 * DA_WIDTH
    ones_bd = _block_diag_ones(DA_WIDTH, DA_HEAD)
    full = lambda shape: pl.BlockSpec(shape, lambda i: (0,) * len(shape))
    return pl.pallas_call(
        _in_proj_kernel,
        grid=(n // tm,),
        in_specs=[pl.BlockSpec((tm, d), lambda i: (i, 0)), full((1, d)), full((d, cols)),
                  full((1, DA_WIDTH)), full((1, DA_WIDTH)), full((DA_WIDTH, DA_WIDTH))],
        out_specs=[pl.BlockSpec((tm, DA_WIDTH), lambda i: (i, 0))] * 2
        + [pl.BlockSpec((DA_WIDTH, tm), lambda i: (0, i)), pl.BlockSpec((tm, zc_w), lambda i: (i, 0))],
        out_shape=[jax.ShapeDtypeStruct((n, DA_WIDTH), BF16)] * 2
        + [jax.ShapeDtypeStruct((DA_WIDTH, n), BF16), jax.ShapeDtypeStruct((n, zc_w), F32)],
        compiler_params=pltpu.CompilerParams(dimension_semantics=("arbitrary",),
                                             vmem_limit_bytes=VMEM_LIMIT_BYTES),
        name="in_proj",
    )(x, g1, w_in_bf, qg, kg, ones_bd)


NEG_BIG = -1e30


def _attn_kernel(lam_ref, sg_ref, q_ref, k_ref, vt_ref, o_ref, acc_ref, m_ref, l_ref, *, lam_init, tq):
    qi = pl.program_id(2)
    lv = lam_ref[...]
    lam = (jnp.exp(jnp.sum(lv[0:1] * lv[1:2], axis=-1, keepdims=True))
           - jnp.exp(jnp.sum(lv[2:3] * lv[3:4], axis=-1, keepdims=True)) + lam_init)

    q = q_ref[...]
    lane = lax.broadcasted_iota(jnp.int32, q.shape, 1)
    zero = jnp.zeros_like(q)
    qs = (jnp.where(lane < DA_HEAD, q, zero), jnp.where(lane >= DA_HEAD, q, zero))

    acc_ref[...] = jnp.zeros_like(acc_ref)
    m_ref[...] = jnp.full_like(m_ref, NEG_BIG)
    l_ref[...] = jnp.zeros_like(l_ref)

    def step(j, masked):
        start = pl.multiple_of(j * tq, tq)
        vtb = vt_ref[:, pl.ds(start, tq)]

        def scores(c):
            s = _dot_nt(k_ref[pl.ds(start, tq), :], qs[c])
            if masked:
                key = lax.broadcasted_iota(jnp.int32, s.shape, 0)
                qry = lax.broadcasted_iota(jnp.int32, s.shape, 1)
                s = jnp.where(key <= qry, s, NEG_BIG)
            return s

        for c in range(2):
            m_old = m_ref[c]
            m_new = jnp.maximum(m_old, jnp.max(scores(c), axis=0, keepdims=True))
            m_ref[c] = m_new
            alpha = jnp.exp2(m_old - m_new)
            p = jnp.exp2(scores(c) - m_new)
            l_ref[c] = alpha * l_ref[c] + jnp.sum(p, axis=0, keepdims=True)
            acc_ref[c] = alpha * acc_ref[c] + _dot(vtb, p.astype(BF16))

    def body(j, carry):
        step(j, False)
        return carry

    lax.fori_loop(0, qi, body, 0)
    step(qi, True)

    o = acc_ref[0] / l_ref[0] - lam * (acc_ref[1] / l_ref[1])
    o = o * lax.rsqrt(jnp.mean(o * o, axis=0, keepdims=True) + RMS_EPS) * sg_ref[...]
    o_ref[...] = (o * (1.0 - lam_init)).T.astype(BF16)


def _attention(q, k, vt, lam_vecs, subln_g, *, lam_init, tq):
    b, s, _ = q.shape
    qspec = pl.BlockSpec((None, tq, DA_VDIM), lambda bi, h, qi: (bi, qi, h))
    return pl.pallas_call(
        functools.partial(_attn_kernel, lam_init=lam_init, tq=tq),
        grid=(b, DA_HEADS, s // tq),
        in_specs=[pl.BlockSpec((4, DA_HEAD), lambda bi, h, qi: (0, 0)),
                  pl.BlockSpec((DA_VDIM, 1), lambda bi, h, qi: (0, 0)),
                  qspec,
                  pl.BlockSpec((None, s, DA_VDIM), lambda bi, h, qi: (bi, 0, h)),
                  pl.BlockSpec((DA_VDIM, s), lambda bi, h, qi: (h, bi))],
        out_specs=qspec,
        out_shape=jax.ShapeDtypeStruct(q.shape, BF16),
        scratch_shapes=[pltpu.VMEM((2, DA_VDIM, tq), F32), pltpu.VMEM((2, 1, tq), F32),
                        pltpu.VMEM((2, 1, tq), F32)],
        compiler_params=pltpu.CompilerParams(dimension_semantics=("arbitrary",) * 3,
                                             vmem_limit_bytes=VMEM_LIMIT_BYTES),
        name="diff_attention",
    )(lam_vecs, subln_g, q, k, vt)


def _out_proj_kernel(a_ref, zc_ref, halo_ref, x_ref, w_ref, dw_ref, dwb_ref, lng_ref, lnb_ref, scw_ref,
                     g2_ref, ones_ref, x1_ref, xn_ref, xnt_ref, ubuf, pbuf, *, tm, tiles_per_seq):
    i = pl.program_id(0)
    keep = jnp.where(i % tiles_per_seq == 0, 0.0, 1.0)
    c1, c2, c3, c4 = CF_WIDTH, 2 * CF_WIDTH, 2 * CF_WIDTH + SC_WIDTH, 2 * CF_WIDTH + 2 * SC_WIDTH

    def glu(z):
        return z[:, :c1] * jax.nn.sigmoid(z[:, c1:c2])

    def cx(z):
        return z[:, c3:c4] * z[:, c4:]

    zc = zc_ref[...]
    halo = halo_ref[...]
    ubuf[0:HALO, :] = glu(halo) * keep
    ubuf[HALO:, :] = glu(zc)
    pbuf[0:HALO, :] = cx(halo) * keep
    pbuf[HALO:, :] = cx(zc)

    dw = dw_ref[...]
    u = jnp.zeros((tm, CF_WIDTH), F32) + dwb_ref[...]
    for j in range(CF_CONV):
        u = u + dw[j:j + 1, :] * ubuf[pl.ds(HALO - (CF_CONV - 1) + j, tm), :]
    ones_bd = ones_ref[...]
    gsz = CF_WIDTH // CF_GROUPS
    mu = _group_sum(u, ones_bd) * (1.0 / gsz)
    uc = u - mu
    var = _group_sum(uc * uc, ones_bd) * (1.0 / gsz)
    un = uc * lax.rsqrt(var + LN_EPS) * lng_ref[...] + lnb_ref[...]
    un = un * jax.nn.sigmoid(un)

    scw = scw_ref[...]
    cc = jnp.zeros((tm, SC_WIDTH), F32)
    for j in range(SC_CONV):
        cc = cc + scw[j:j + 1, :] * pbuf[pl.ds(HALO - (SC_CONV - 1) + j, tm), :]
    cc = zc[:, c2:c3] * cc

    y = (_dot(a_ref[...], w_ref[0:DA_WIDTH, :])
         + _dot(un.astype(BF16), w_ref[DA_WIDTH:DA_WIDTH + CF_WIDTH, :])
         + _dot(cc.astype(BF16), w_ref[DA_WIDTH + CF_WIDTH:, :]))
    x1 = x_ref[...] + y
    x1_ref[...] = x1
    xn = x1 * lax.rsqrt(jnp.mean(x1 * x1, axis=-1, keepdims=True) + RMS_EPS) * g2_ref[...]
    xn_ref[...] = xn.astype(BF16)
    xnt_ref[...] = xn.T.astype(BF16)


def _out_proj(a, zc, x, w_out_bf, cf_dw, cf_dw_b, cf_ln_g, cf_ln_b, sc_w, g2, *, tm, seq):
    n, d = x.shape
    zc_w = zc.shape[1]
    hb = tm // HALO
    ones_bd = _block_diag_ones(CF_WIDTH, CF_WIDTH // CF_GROUPS)
    full = lambda shape: pl.BlockSpec(shape, lambda i: (0,) * len(shape))
    row = lambda w: pl.BlockSpec((tm, w), lambda i: (i, 0))
    return pl.pallas_call(
        functools.partial(_out_proj_kernel, tm=tm, tiles_per_seq=seq // tm),
        grid=(n // tm,),
        in_specs=[row(DA_WIDTH), row(zc_w),
                  pl.BlockSpec((HALO, zc_w), lambda i: (jnp.maximum(i * hb - 1, 0), 0)),
                  row(d), full((d, d)), full((CF_CONV, CF_WIDTH)), full((1, CF_WIDTH)),
                  full((1, CF_WIDTH)), full((1, CF_WIDTH)), full((SC_CONV, SC_WIDTH)), full((1, d)),
                  full((CF_WIDTH, CF_WIDTH))],
        out_specs=[row(d), row(d), pl.BlockSpec((d, tm), lambda i: (0, i))],
        out_shape=[jax.ShapeDtypeStruct((n, d), F32), jax.ShapeDtypeStruct((n, d), BF16),
                   jax.ShapeDtypeStruct((d, n), BF16)],
        scratch_shapes=[pltpu.VMEM((tm + HALO, CF_WIDTH), F32), pltpu.VMEM((tm + HALO, SC_WIDTH), F32)],
        compiler_params=pltpu.CompilerParams(dimension_semantics=("arbitrary",),
                                             vmem_limit_bytes=VMEM_LIMIT_BYTES),
        name="out_proj",
    )(a, zc, zc, x, w_out_bf, cf_dw, cf_dw_b, cf_ln_g, cf_ln_b, sc_w, g2, ones_bd)


def _oddeven_merge_sort_pairs(n):
    pairs = []
    p = 1
    while p < n:
        k = p
        while k >= 1:
            for j in range(k % p, n - k, 2 * k):
                for i in range(min(k, n - j - k)):
                    if (i + j) // (2 * p) == (i + j + k) // (2 * p):
                        pairs.append((i + j, i + j + k))
            k //= 2
        p *= 2
    return pairs


def _bitonic_sort_pairs(n):
    pairs = []
    k = n // 2
    while k >= 1:
        for i in range(n):
            if (i // k) % 2 == 0:
                pairs.append((i, i + k))
        k //= 2
    return pairs


SORT16_PAIRS = _oddeven_merge_sort_pairs(PEER_TOPK)
BITONIC16_PAIRS = _bitonic_sort_pairs(PEER_TOPK)
CAND_CELLS = [(j, k) for j in range(PEER_TOPK) for k in range(PEER_TOPK) if (j + 1) * (k + 1) <= PEER_TOPK]


def _beats(a, b):
    return (a[0] > b[0]) | ((a[0] == b[0]) & (a[1] < b[1]))


def _cmp_exchange(items, i, j):
    a, b = items[i], items[j]
    sw = _beats(b, a)
    items[i] = (jnp.where(sw, b[0], a[0]), jnp.where(sw, b[1], a[1]))
    items[j] = (jnp.where(sw, a[0], b[0]), jnp.where(sw, a[1], b[1]))


def _merge_top(a, b):
    out = []
    for i in range(PEER_TOPK):
        x, y = a[i], b[PEER_TOPK - 1 - i]
        sw = _beats(y, x)
        out.append((jnp.where(sw, y[0], x[0]), jnp.where(sw, y[1], x[1])))
    for (i, j) in BITONIC16_PAIRS:
        _cmp_exchange(out, i, j)
    return out


def _peer_route_kernel(xn_ref, wq_ref, keys_ref, r2_ref, e2_ref, n1_ref, e1_ref,
                       sbuf, sval, sidx, obuf, *, tr):
    ng = tr // LANES
    assert ng == SUBLANES
    q = _dot(xn_ref[...], wq_ref[...]).astype(BF16)
    keys = keys_ref[...].astype(BF16)
    for c in range(2):
        st = _dot_nt(keys[c], q[:, c * PEER_HALF:(c + 1) * PEER_HALF])
        for g in range(ng):
            sbuf[c, g * N_KEYS:(g + 1) * N_KEYS, :] = st[:, g * LANES:(g + 1) * LANES]

    def key_row(c, n):
        return sbuf[c, pl.ds(n, ng, stride=N_KEYS), :]

    n_groups = N_KEYS // PEER_TOPK

    def sort_group(t, carry):
        c = t // n_groups
        grp = t % n_groups
        base = grp * PEER_TOPK
        items = []
        for r in range(PEER_TOPK):
            val = key_row(c, base + r)
            idx = jnp.full(val.shape, r, jnp.int32) + base
            items.append((val, idx))
        for (i, j) in SORT16_PAIRS:
            _cmp_exchange(items, i, j)
        for r in range(PEER_TOPK):
            sval[t, r] = items[r][0]
            sidx[t, r] = items[r][1]
        return carry

    lax.fori_loop(0, 2 * n_groups, sort_group, 0)

    tops = []
    for c in range(2):
        lists = [[(sval[c * n_groups + g, r], sidx[c * n_groups + g, r]) for r in range(PEER_TOPK)]
                 for g in range(n_groups)]
        while len(lists) > 1:
            lists = [_merge_top(lists[2 * m], lists[2 * m + 1]) for m in range(len(lists) // 2)]
        tops.append(lists[0])
    (v1, i1), (v2, i2) = [([p[0] for p in t], [p[1] for p in t]) for t in tops]

    cand = {cell: v1[cell[0]] + v2[cell[1]] for cell in CAND_CELLS}
    rank = {cell: jnp.zeros(v1[0].shape, F32) for cell in CAND_CELLS}
    for pi, p in enumerate(CAND_CELLS):
        for qcell in CAND_CELLS[pi + 1:]:
            if p[0] <= qcell[0] and p[1] <= qcell[1]:
                rank[qcell] = rank[qcell] + 1.0
            else:
                ge = jnp.where(cand[p] >= cand[qcell], 1.0, 0.0)
                rank[qcell] = rank[qcell] + ge
                rank[p] = rank[p] + (1.0 - ge)
    sel = {cell: jnp.where(rank[cell] < PEER_TOPK, 1.0, 0.0) for cell in CAND_CELLS}
    cnt = [sum(sel[(j, k)] for k in range(PEER_TOPK) if (j, k) in sel) for j in range(PEER_TOPK)]

    ej = [jnp.exp(v1[j] - v1[0]) for j in range(PEER_TOPK)]
    ek = [jnp.exp(v2[k] - v2[0]) for k in range(PEER_TOPK)]
    zsum = sum(sel[(j, k)] * (ej[j] * ek[k]) for (j, k) in CAND_CELLS)
    zinv = 1.0 / zsum

    def emit(out_ref, rows):
        for n in range(N_KEYS):
            obuf[pl.ds(n, ng, stride=N_KEYS), :] = rows(n)
        for g in range(ng):
            blk = obuf[g * N_KEYS:(g + 1) * N_KEYS, :].astype(out_ref.dtype)
            if len(out_ref.shape) == 3:
                out_ref[g] = blk
            else:
                out_ref[:, g * LANES:(g + 1) * LANES] = blk

    def rank2_row(n):
        acc = jnp.full(v1[0].shape, float(PEER_TOPK), F32)
        for k in range(PEER_TOPK):
            acc = jnp.where(i2[k] == n, float(k), acc)
        return acc

    def count1_row(n):
        acc = jnp.zeros(v1[0].shape, F32)
        for j in range(PEER_TOPK):
            acc = jnp.where(i1[j] == n, cnt[j], acc)
        return acc

    emit(r2_ref, rank2_row)
    emit(n1_ref, count1_row)
    emit(e2_ref, lambda n: jnp.exp(key_row(1, n) - v2[0]))
    emit(e1_ref, lambda n: jnp.exp(key_row(0, n) - v1[0]) * zinv)


def _peer_route(xn, wq_bf, keys, *, tr):
    n, d = xn.shape
    hq = wq_bf.shape[1] // PEER_HEADS
    ng = tr // LANES
    spec2 = pl.BlockSpec((None, N_KEYS, tr), lambda i, h: (h, 0, i))
    spec1 = pl.BlockSpec((None, ng, N_KEYS, LANES), lambda i, h: (h, i, 0, 0))
    shape2 = jax.ShapeDtypeStruct((PEER_HEADS, N_KEYS, n), BF16)
    shape1 = jax.ShapeDtypeStruct((PEER_HEADS, n // LANES, N_KEYS, LANES), F32)
    n_lists = 2 * (N_KEYS // PEER_TOPK)
    return pl.pallas_call(
        functools.partial(_peer_route_kernel, tr=tr),
        grid=(n // tr, PEER_HEADS),
        in_specs=[pl.BlockSpec((tr, d), lambda i, h: (i, 0)),
                  pl.BlockSpec((d, hq), lambda i, h: (0, h)),
                  pl.BlockSpec((None, 2, N_KEYS, PEER_HALF), lambda i, h: (h, 0, 0, 0))],
        out_specs=[spec2, spec2, spec1, spec1],
        out_shape=[shape2, shape2, shape1, shape1],
        scratch_shapes=[pltpu.VMEM((2, ng * N_KEYS, LANES), F32),
                        pltpu.VMEM((n_lists, PEER_TOPK, ng, LANES), F32),
                        pltpu.VMEM((n_lists, PEER_TOPK, ng, LANES), jnp.int32),
                        pltpu.VMEM((ng * N_KEYS, LANES), F32)],
        compiler_params=pltpu.CompilerParams(dimension_semantics=("arbitrary",) * 2,
                                             vmem_limit_bytes=VMEM_LIMIT_BYTES),
        name="peer_route",
    )(xn, wq_bf, keys)


def _gelu_x2(x):
    return x * (1.0 + lax.erf(x * (2.0 ** -0.5)))


BF16_ROWS = 2 * SUBLANES


def _peer_dense_kernel(xnt_ref, u_ref, vt_ref, r2_ref, e2_ref, n1_ref, e1_ref, x1_ref, o_ref,
                       acc_ref, act_ref, w_ref, *, et, tc):
    e = pl.program_id(1)
    ne = pl.num_programs(1)
    a_per = et // N_KEYS
    tt = xnt_ref.shape[1]
    n_chunks = tt // tc
    packed = (N_KEYS // BF16_ROWS, BF16_ROWS, tc)

    @pl.when(e == 0)
    def _():
        acc_ref[...] = jnp.zeros_like(acc_ref)

    def first_key_row(ref, h, arow, c):
        groups = range(c * tc // LANES, (c + 1) * tc // LANES)
        parts = [ref[h, g, pl.ds(arow, BF16_ROWS, stride=0), :] for g in groups]
        return jnp.concatenate(parts, axis=1).astype(BF16)[None]

    def pre_activation(c):
        return _dot(u_ref[...], xnt_ref[:, c * tc:(c + 1) * tc])

    act_ref[0] = pre_activation(0)
    for c in range(n_chunks):
        cols = slice(c * tc, (c + 1) * tc)
        if c + 1 < n_chunks:
            act_ref[(c + 1) % 2] = pre_activation(c + 1)
        for al in range(a_per):
            arow = e * a_per + al
            rows = slice(al * N_KEYS, (al + 1) * N_KEYS)
            gate = jnp.zeros(packed, BF16)
            for h in range(PEER_HEADS):
                n1 = first_key_row(n1_ref, h, arow, c)
                e1 = first_key_row(e1_ref, h, arow, c)
                r2 = r2_ref[h, :, cols].reshape(packed)
                e2 = e2_ref[h, :, cols].reshape(packed)
                gate = gate + jnp.where(r2 < n1, e2, jnp.zeros((), BF16)) * e1
            g = _gelu_x2(act_ref[c % 2, rows, :]).astype(BF16)
            w_ref[c, rows, :] = g * gate.reshape(N_KEYS, tc)
        acc_ref[:, cols] += _dot(vt_ref[...], w_ref[c])

    @pl.when(e == ne - 1)
    def _():
        o_ref[...] = x1_ref[...] + acc_ref[...].T


def _peer_dense(xnt, u_bf, vt_bf, maps, x1, *, tt, et, tc):
    d, n = xnt.shape
    n_exp = u_bf.shape[0]
    once = pl.Buffered(1)
    mapspec = pl.BlockSpec((PEER_HEADS, N_KEYS, tt), lambda i, e: (0, 0, i), pipeline_mode=once)
    rowspec = pl.BlockSpec((PEER_HEADS, tt // LANES, N_KEYS, LANES), lambda i, e: (0, i, 0, 0),
                           pipeline_mode=once)
    return pl.pallas_call(
        functools.partial(_peer_dense_kernel, et=et, tc=tc),
        grid=(n // tt, n_exp // et),
        in_specs=[pl.BlockSpec((d, tt), lambda i, e: (0, i), pipeline_mode=once),
                  pl.BlockSpec((et, d), lambda i, e: (e, 0)),
                  pl.BlockSpec((d, et), lambda i, e: (0, e)),
                  mapspec, mapspec, rowspec, rowspec,
                  pl.BlockSpec((tt, d), lambda i, e: (i, 0), pipeline_mode=once)],
        out_specs=pl.BlockSpec((tt, d), lambda i, e: (i, 0)),
        out_shape=jax.ShapeDtypeStruct((n, d), F32),
        scratch_shapes=[pltpu.VMEM((d, tt), F32), pltpu.VMEM((2, et, tc), F32),
                        pltpu.VMEM((tt // tc, et, tc), BF16)],
        compiler_params=pltpu.CompilerParams(dimension_semantics=("arbitrary",) * 2,
                                             vmem_limit_bytes=VMEM_LIMIT_BYTES),
        name="peer_dense",
    )(xnt, u_bf, vt_bf, *maps, x1)


def _tiles(n_tokens, seq):
    tm = min(512, seq)
    tq = min(512, seq)
    tr = SUBLANES * LANES
    tt = min(1024, n_tokens)
    et = 1024
    tc = 256
    return tm, tq, tr, tt, et, tc


def _layer(x2d, batch, seq, lam_init, p):
    n, d = x2d.shape
    tm, tq, tr, tt, et, tc = _tiles(n, seq)
    row = lambda v: v.reshape(1, -1)
    q, k, vt, zc = _in_proj(x2d, row(p["norm1_g"]), p["w_in"].astype(BF16),
                            row(jnp.tile(p["q_norm_g"], DA_WIDTH // DA_HEAD)),
                            row(jnp.tile(p["k_norm_g"], DA_WIDTH // DA_HEAD)), tm=tm)
    lam_vecs = jnp.stack([p["lam_q1"], p["lam_k1"], p["lam_q2"], p["lam_k2"]])
    shp = (batch, seq, DA_WIDTH)
    a = _attention(q.reshape(shp), k.reshape(shp), vt, lam_vecs, p["subln_g"].reshape(-1, 1),
                   lam_init=lam_init, tq=tq).reshape(n, DA_WIDTH)
    x1, xn, xnt = _out_proj(a, zc, x2d, p["w_out"].astype(BF16), p["cf_dw"], row(p["cf_dw_b"]),
                            row(p["cf_ln_g"]), row(p["cf_ln_b"]), p["sc_w"], row(p["norm2_g"]), tm=tm, seq=seq)
    maps = _peer_route(xn, p["peer_wq"].astype(BF16), p["peer_keys"], tr=tr)
    half_vt = (0.5 * p["peer_v"]).astype(BF16).T
    return _peer_dense(xnt, p["peer_u"].astype(BF16), half_vt, maps, x1,
                       tt=tt, et=et, tc=tc)


def kernel(x, norm1_g, w_in, q_norm_g, k_norm_g, lam_q1, lam_k1, lam_q2, lam_k2, subln_g, cf_dw, cf_dw_b,
           cf_ln_g, cf_ln_b, sc_w, w_out, norm2_g, peer_wq, peer_keys, peer_u, peer_v):
    params = dict(norm1_g=norm1_g, w_in=w_in, q_norm_g=q_norm_g, k_norm_g=k_norm_g, lam_q1=lam_q1,
                  lam_k1=lam_k1, lam_q2=lam_q2, lam_k2=lam_k2, subln_g=subln_g, cf_dw=cf_dw,
                  cf_dw_b=cf_dw_b, cf_ln_g=cf_ln_g, cf_ln_b=cf_ln_b, sc_w=sc_w, w_out=w_out,
                  norm2_g=norm2_g, peer_wq=peer_wq, peer_keys=peer_keys, peer_u=peer_u, peer_v=peer_v)
    batch, seq, d = x.shape
    x2d = x.reshape(batch * seq, d)
    for l in range(DEPTH):
        lam_init = 0.8 - 0.6 * math.exp(-0.3 * l)
        x2d = _layer(x2d, batch, seq, lam_init, {name: val[l] for name, val in params.items()})
    return x2d.reshape(batch, seq, d)
```

```python
import functools
import math

import jax
import jax.numpy as jnp
from jax import lax
from jax.experimental import pallas as pl
from jax.experimental.pallas import tpu as pltpu

F32 = jnp.float32
BF16 = jnp.bfloat16

DEPTH = 2
RMS_EPS = 1e-6
LN_EPS = 1e-5
LOG2_E = math.log2(math.e)
DA_HEAD = 64
DA_VDIM = 128
DA_HEADS = 4
DA_WIDTH = 512
CF_WIDTH = 256
CF_GROUPS = 4
CF_CONV = 31
SC_WIDTH = 256
SC_CONV = 3
PEER_HEADS = 8
N_KEYS = 128
PEER_TOPK = 16
PEER_HALF = 128
KEY_PITCH = N_KEYS + 8

LANES = 128
SUBLANES = 8
VMEM_LIMIT_BYTES = 56 * 1024 * 1024

HALO = 32

NT_DIMS = (((1,), (1,)), ((), ()))


def _dot(a, b):
    return jnp.dot(a, b, preferred_element_type=F32)


def _dot_nt(a, b):
    return lax.dot_general(a, b, NT_DIMS, preferred_element_type=F32)


def _group_sum(v, ones_bd):
    hi = v.astype(BF16)
    lo = (v - hi.astype(F32)).astype(BF16)
    return _dot(hi, ones_bd) + _dot(lo, ones_bd)


def _block_diag_ones(width, group):
    r = jnp.arange(width) // group
    return (r[:, None] == r[None, :]).astype(BF16)


def _in_proj_kernel(x_ref, g1_ref, w_ref, qg_ref, kg_ref, ones_ref, q_ref, k_ref, vt_ref, zc_ref):
    x = x_ref[...]
    xn = x * lax.rsqrt(jnp.mean(x * x, axis=-1, keepdims=True) + RMS_EPS) * g1_ref[...]
    z = _dot(xn.astype(BF16), w_ref[...])
    ones_bd = ones_ref[...]

    def qk_norm(t, g):
        ms = _group_sum(t * t, ones_bd) * (1.0 / DA_HEAD)
        return t * lax.rsqrt(ms + RMS_EPS) * g

    o1, o2, o3 = DA_WIDTH, 2 * DA_WIDTH, 3 * DA_WIDTH
    q_ref[...] = (qk_norm(z[:, :o1], qg_ref[...]) * (DA_HEAD ** -0.5 * LOG2_E)).astype(BF16)
    k_ref[...] = qk_norm(z[:, o1:o2], kg_ref[...]).astype(BF16)
    vt_ref[...] = z[:, o2:o3].T.astype(BF16)
    zc_ref[...] = z[:, o3:]


def _in_proj(x, g1, w_in_bf, qg, kg, *, tm):
    n, d = x.shape
    cols = w_in_bf.shape[1]
    zc_w = cols - 3 * DA_WIDTH
    ones_bd = _block_diag_ones(DA_WIDTH, DA_HEAD)
    full = lambda shape: pl.BlockSpec(shape, lambda i: (0,) * len(shape))
    return pl.pallas_call(
        _in_proj_kernel,
        grid=(n // tm,),
        in_specs=[pl.BlockSpec((tm, d), lambda i: (i, 0)), full((1, d)), full((d, cols)),
                  full((1, DA_WIDTH)), full((1, DA_WIDTH)), full((DA_WIDTH, DA_WIDTH))],
        out_specs=[pl.BlockSpec((tm, DA_WIDTH), lambda i: (i, 0))] * 2
        + [pl.BlockSpec((DA_WIDTH, tm), lambda i: (0, i)), pl.BlockSpec((tm, zc_w), lambda i: (i, 0))],
        out_shape=[jax.ShapeDtypeStruct((n, DA_WIDTH), BF16)] * 2
        + [jax.ShapeDtypeStruct((DA_WIDTH, n), BF16), jax.ShapeDtypeStruct((n, zc_w), F32)],
        compiler_params=pltpu.CompilerParams(dimension_semantics=("arbitrary",),
                                             vmem_limit_bytes=VMEM_LIMIT_BYTES),
        name="in_proj",
    )(x, g1, w_in_bf, qg, kg, ones_bd)


NEG_BIG = -1e30


def _attn_kernel(lam_ref, sg_ref, q_ref, k_ref, vt_ref, o_ref, acc_ref, m_ref, l_ref, *, lam_init, tq):
    qi = pl.program_id(2)
    lv = lam_ref[...]
    lam = (jnp.exp(jnp.sum(lv[0:1] * lv[1:2], axis=-1, keepdims=True))
           - jnp.exp(jnp.sum(lv[2:3] * lv[3:4], axis=-1, keepdims=True)) + lam_init)

    q = q_ref[...]
    lane = lax.broadcasted_iota(jnp.int32, q.shape, 1)
    zero = jnp.zeros_like(q)
    qs = (jnp.where(lane < DA_HEAD, q, zero), jnp.where(lane >= DA_HEAD, q, zero))

    acc_ref[...] = jnp.zeros_like(acc_ref)
    m_ref[...] = jnp.full_like(m_ref, NEG_BIG)
    l_ref[...] = jnp.zeros_like(l_ref)

    def step(j, masked):
        start = pl.multiple_of(j * tq, tq)
        vtb = vt_ref[:, pl.ds(start, tq)]

        def scores(c):
            s = _dot_nt(k_ref[pl.ds(start, tq), :], qs[c])
            if masked:
                key = lax.broadcasted_iota(jnp.int32, s.shape, 0)
                qry = lax.broadcasted_iota(jnp.int32, s.shape, 1)
                s = jnp.where(key <= qry, s, NEG_BIG)
            return s

        for c in range(2):
            m_old = m_ref[c]
            m_new = jnp.maximum(m_old, jnp.max(scores(c), axis=0, keepdims=True))
            m_ref[c] = m_new
            alpha = jnp.exp2(m_old - m_new)
            p = jnp.exp2(scores(c) - m_new)
            l_ref[c] = alpha * l_ref[c] + jnp.sum(p, axis=0, keepdims=True)
            acc_ref[c] = alpha * acc_ref[c] + _dot(vtb, p.astype(BF16))

    def body(j, carry):
        step(j, False)
        return carry

    lax.fori_loop(0, qi, body, 0)
    step(qi, True)

    o = acc_ref[0] / l_ref[0] - lam * (acc_ref[1] / l_ref[1])
    o = o * lax.rsqrt(jnp.mean(o * o, axis=0, keepdims=True) + RMS_EPS) * sg_ref[...]
    o_ref[...] = (o * (1.0 - lam_init)).T.astype(BF16)


def _attention(q, k, vt, lam_vecs, subln_g, *, lam_init, tq):
    b, s, _ = q.shape
    qspec = pl.BlockSpec((None, tq, DA_VDIM), lambda bi, h, qi: (bi, qi, h))
    return pl.pallas_call(
        functools.partial(_attn_kernel, lam_init=lam_init, tq=tq),
        grid=(b, DA_HEADS, s // tq),
        in_specs=[pl.BlockSpec((4, DA_HEAD), lambda bi, h, qi: (0, 0)),
                  pl.BlockSpec((DA_VDIM, 1), lambda bi, h, qi: (0, 0)),
                  qspec,
                  pl.BlockSpec((None, s, DA_VDIM), lambda bi, h, qi: (bi, 0, h)),
                  pl.BlockSpec((DA_VDIM, s), lambda bi, h, qi: (h, bi))],
        out_specs=qspec,
        out_shape=jax.ShapeDtypeStruct(q.shape, BF16),
        scratch_shapes=[pltpu.VMEM((2, DA_VDIM, tq), F32), pltpu.VMEM((2, 1, tq), F32),
                        pltpu.VMEM((2, 1, tq), F32)],
        compiler_params=pltpu.CompilerParams(dimension_semantics=("arbitrary",) * 3,
                                             vmem_limit_bytes=VMEM_LIMIT_BYTES),
        name="diff_attention",
    )(lam_vecs, subln_g, q, k, vt)


def _out_proj_kernel(a_ref, zc_ref, halo_ref, x_ref, w_ref, dw_ref, dwb_ref, lng_ref, lnb_ref, scw_ref,
                     g2_ref, ones_ref, x1_ref, xn_ref, xnt_ref, ubuf, pbuf, *, tm, tiles_per_seq):
    i = pl.program_id(0)
    keep = jnp.where(i % tiles_per_seq == 0, 0.0, 1.0)
    c1, c2, c3, c4 = CF_WIDTH, 2 * CF_WIDTH, 2 * CF_WIDTH + SC_WIDTH, 2 * CF_WIDTH + 2 * SC_WIDTH

    def glu(z):
        return z[:, :c1] * jax.nn.sigmoid(z[:, c1:c2])

    def cx(z):
        return z[:, c3:c4] * z[:, c4:]

    zc = zc_ref[...]
    halo = halo_ref[...]
    ubuf[0:HALO, :] = glu(halo) * keep
    ubuf[HALO:, :] = glu(zc)
    pbuf[0:HALO, :] = cx(halo) * keep
    pbuf[HALO:, :] = cx(zc)

    dw = dw_ref[...]
    u = jnp.zeros((tm, CF_WIDTH), F32) + dwb_ref[...]
    for j in range(CF_CONV):
        u = u + dw[j:j + 1, :] * ubuf[pl.ds(HALO - (CF_CONV - 1) + j, tm), :]
    ones_bd = ones_ref[...]
    gsz = CF_WIDTH // CF_GROUPS
    mu = _group_sum(u, ones_bd) * (1.0 / gsz)
    uc = u - mu
    var = _group_sum(uc * uc, ones_bd) * (1.0 / gsz)
    un = uc * lax.rsqrt(var + LN_EPS) * lng_ref[...] + lnb_ref[...]
    un = un * jax.nn.sigmoid(un)

    scw = scw_ref[...]
    cc = jnp.zeros((tm, SC_WIDTH), F32)
    for j in range(SC_CONV):
        cc = cc + scw[j:j + 1, :] * pbuf[pl.ds(HALO - (SC_CONV - 1) + j, tm), :]
    cc = zc[:, c2:c3] * cc

    y = (_dot(a_ref[...], w_ref[0:DA_WIDTH, :])
         + _dot(un.astype(BF16), w_ref[DA_WIDTH:DA_WIDTH + CF_WIDTH, :])
         + _dot(cc.astype(BF16), w_ref[DA_WIDTH + CF_WIDTH:, :]))
    x1 = x_ref[...] + y
    x1_ref[...] = x1
    xn = x1 * lax.rsqrt(jnp.mean(x1 * x1, axis=-1, keepdims=True) + RMS_EPS) * g2_ref[...]
    xn_ref[...] = xn.astype(BF16)
    xnt_ref[...] = xn.T.astype(BF16)


def _out_proj(a, zc, x, w_out_bf, cf_dw, cf_dw_b, cf_ln_g, cf_ln_b, sc_w, g2, *, tm, seq):
    n, d = x.shape
    zc_w = zc.shape[1]
    hb = tm // HALO
    ones_bd = _block_diag_ones(CF_WIDTH, CF_WIDTH // CF_GROUPS)
    full = lambda shape: pl.BlockSpec(shape, lambda i: (0,) * len(shape))
    row = lambda w: pl.BlockSpec((tm, w), lambda i: (i, 0))
    return pl.pallas_call(
        functools.partial(_out_proj_kernel, tm=tm, tiles_per_seq=seq // tm),
        grid=(n // tm,),
        in_specs=[row(DA_WIDTH), row(zc_w),
                  pl.BlockSpec((HALO, zc_w), lambda i: (jnp.maximum(i * hb - 1, 0), 0)),
                  row(d), full((d, d)), full((CF_CONV, CF_WIDTH)), full((1, CF_WIDTH)),
                  full((1, CF_WIDTH)), full((1, CF_WIDTH)), full((SC_CONV, SC_WIDTH)), full((1, d)),
                  full((CF_WIDTH, CF_WIDTH))],
        out_specs=[row(d), row(d), pl.BlockSpec((d, tm), lambda i: (0, i))],
        out_shape=[jax.ShapeDtypeStruct((n, d), F32), jax.ShapeDtypeStruct((n, d), BF16),
                   jax.ShapeDtypeStruct((d, n), BF16)],
        scratch_shapes=[pltpu.VMEM((tm + HALO, CF_WIDTH), F32), pltpu.VMEM((tm + HALO, SC_WIDTH), F32)],
        compiler_params=pltpu.CompilerParams(dimension_semantics=("arbitrary",),
                                             vmem_limit_bytes=VMEM_LIMIT_BYTES),
        name="out_proj",
    )(a, zc, zc, x, w_out_bf, cf_dw, cf_dw_b, cf_ln_g, cf_ln_b, sc_w, g2, ones_bd)


def _oddeven_merge_sort_pairs(n):
    pairs = []
    p = 1
    while p < n:
        k = p
        while k >= 1:
            for j in range(k % p, n - k, 2 * k):
                for i in range(min(k, n - j - k)):
                    if (i + j) // (2 * p) == (i + j + k) // (2 * p):
                        pairs.append((i + j, i + j + k))
            k //= 2
        p *= 2
    return pairs


def _bitonic_sort_pairs(n):
    pairs = []
    k = n // 2
    while k >= 1:
        for i in range(n):
            if (i // k) % 2 == 0:
                pairs.append((i, i + k))
        k //= 2
    return pairs


SORT16_PAIRS = _oddeven_merge_sort_pairs(PEER_TOPK)
BITONIC16_PAIRS = _bitonic_sort_pairs(PEER_TOPK)
CAND_CELLS = [(j, k) for j in range(PEER_TOPK) for k in range(PEER_TOPK) if (j + 1) * (k + 1) <= PEER_TOPK]


def _beats(a, b):
    return (a[0] > b[0]) | ((a[0] == b[0]) & (a[1] < b[1]))


def _cmp_exchange(items, i, j, exact):
    a, b = items[i], items[j]
    if exact:
        sw = _beats(b, a)
        items[i] = (jnp.where(sw, b[0], a[0]), jnp.where(sw, b[1], a[1]))
        items[j] = (jnp.where(sw, a[0], b[0]), jnp.where(sw, a[1], b[1]))
    else:
        sw = b[0] > a[0]
        items[i] = (jnp.maximum(a[0], b[0]), jnp.where(sw, b[1], a[1]))
        items[j] = (jnp.minimum(a[0], b[0]), jnp.where(sw, a[1], b[1]))


def _merge_top(a, b, exact):
    out = []
    for i in range(PEER_TOPK):
        x, y = a[i], b[PEER_TOPK - 1 - i]
        if exact:
            sw = _beats(y, x)
            out.append((jnp.where(sw, y[0], x[0]), jnp.where(sw, y[1], x[1])))
        else:
            out.append((jnp.maximum(x[0], y[0]), jnp.where(y[0] > x[0], y[1], x[1])))
    for (i, j) in BITONIC16_PAIRS:
        _cmp_exchange(out, i, j, exact)
    return out


def _peer_route_kernel(xn_ref, wq_ref, keys_ref, r2_ref, e2_ref, n1_ref, e1_ref,
                       sbuf, sval, sidx, topv, topi, obuf, *, tr):
    ng = tr // LANES
    assert ng == SUBLANES
    q = _dot(xn_ref[...], wq_ref[...]).astype(BF16)
    keys = keys_ref[...].astype(BF16)
    for c in range(2):
        st = _dot_nt(keys[c], q[:, c * PEER_HALF:(c + 1) * PEER_HALF])
        for g in range(ng):
            sbuf[c, g * KEY_PITCH:g * KEY_PITCH + N_KEYS, :] = st[:, g * LANES:(g + 1) * LANES]

    def key_row(c, n):
        return sbuf[c, pl.ds(n, ng, stride=KEY_PITCH), :]

    n_groups = N_KEYS // PEER_TOPK

    def top_lists(exact):
        def sort_group(t, carry):
            c = t // n_groups
            grp = t % n_groups
            base = grp * PEER_TOPK
            items = []
            for r in range(PEER_TOPK):
                val = key_row(c, base + r)
                idx = jnp.full(val.shape, r, jnp.int32) + base
                items.append((val, idx))
            for (i, j) in SORT16_PAIRS:
                _cmp_exchange(items, i, j, exact)
            for r in range(PEER_TOPK):
                sval[t, r] = items[r][0]
                sidx[t, r] = items[r][1]
            return carry

        lax.fori_loop(0, 2 * n_groups, sort_group, 0)

        for c in range(2):
            lists = [[(sval[c * n_groups + g, r], sidx[c * n_groups + g, r]) for r in range(PEER_TOPK)]
                     for g in range(n_groups)]
            while len(lists) > 1:
                lists = [_merge_top(lists[2 * m], lists[2 * m + 1], exact) for m in range(len(lists) // 2)]
            for r in range(PEER_TOPK):
                topv[c, r] = lists[0][r][0]
                topi[c, r] = lists[0][r][1]

    top_lists(exact=False)
    doubt = jnp.zeros((ng, LANES), F32)
    for c in range(2):
        kept = [topv[c, r] for r in range(PEER_TOPK)]
        for r in range(PEER_TOPK - 1):
            doubt = jnp.maximum(doubt, jnp.where(kept[r] == kept[r + 1], 1.0, 0.0))
        at_least = sum(jnp.where(key_row(c, n) >= kept[PEER_TOPK - 1], 1.0, 0.0) for n in range(N_KEYS))
        doubt = jnp.maximum(doubt, jnp.where(at_least != float(PEER_TOPK), 1.0, 0.0))

    @pl.when(jnp.max(doubt) > 0.0)
    def _():
        top_lists(exact=True)

    (v1, i1), (v2, i2) = [([topv[c, r] for r in range(PEER_TOPK)], [topi[c, r] for r in range(PEER_TOPK)])
                          for c in range(2)]

    cand = {cell: v1[cell[0]] + v2[cell[1]] for cell in CAND_CELLS}
    rank = {cell: jnp.zeros(v1[0].shape, F32) for cell in CAND_CELLS}
    for pi, p in enumerate(CAND_CELLS):
        for qcell in CAND_CELLS[pi + 1:]:
            if p[0] <= qcell[0] and p[1] <= qcell[1]:
                rank[qcell] = rank[qcell] + 1.0
            else:
                ge = jnp.where(cand[p] >= cand[qcell], 1.0, 0.0)
                rank[qcell] = rank[qcell] + ge
                rank[p] = rank[p] + (1.0 - ge)
    sel = {cell: jnp.where(rank[cell] < PEER_TOPK, 1.0, 0.0) for cell in CAND_CELLS}
    cnt = [sum(sel[(j, k)] for k in range(PEER_TOPK) if (j, k) in sel) for j in range(PEER_TOPK)]

    ej = [jnp.exp(v1[j] - v1[0]) for j in range(PEER_TOPK)]
    ek = [jnp.exp(v2[k] - v2[0]) for k in range(PEER_TOPK)]
    zsum = sum(sel[(j, k)] * (ej[j] * ek[k]) for (j, k) in CAND_CELLS)
    zinv = 1.0 / zsum

    def emit(out_ref, rows):
        for n in range(N_KEYS):
            obuf[pl.ds(n, ng, stride=KEY_PITCH), :] = rows(n)
        for g in range(ng):
            out_ref[:, g * LANES:(g + 1) * LANES] = obuf[g * KEY_PITCH:g * KEY_PITCH + N_KEYS, :].astype(out_ref.dtype)

    def rank2_row(n):
        acc = jnp.full(v1[0].shape, float(PEER_TOPK), F32)
        for k in range(PEER_TOPK):
            acc = jnp.where(i2[k] == n, float(k), acc)
        return acc

    def count1_row(n):
        acc = jnp.zeros(v1[0].shape, F32)
        for j in range(PEER_TOPK):
            acc = jnp.where(i1[j] == n, cnt[j], acc)
        return acc

    emit(r2_ref, rank2_row)
    emit(n1_ref, count1_row)
    emit(e2_ref, lambda n: jnp.exp(key_row(1, n) - v2[0]))
    emit(e1_ref, lambda n: jnp.exp(key_row(0, n) - v1[0]) * zinv)


def _peer_route(xn, wq_bf, keys, *, tr):
    n, d = xn.shape
    hq = wq_bf.shape[1] // PEER_HEADS
    ng = tr // LANES
    mapspec = pl.BlockSpec((None, N_KEYS, tr), lambda i, h: (h, 0, i))
    mapshapes = [jax.ShapeDtypeStruct((PEER_HEADS, N_KEYS, n), dt) for dt in (BF16, BF16, F32, F32)]
    n_lists = 2 * (N_KEYS // PEER_TOPK)
    return pl.pallas_call(
        functools.partial(_peer_route_kernel, tr=tr),
        grid=(n // tr, PEER_HEADS),
        in_specs=[pl.BlockSpec((tr, d), lambda i, h: (i, 0)),
                  pl.BlockSpec((d, hq), lambda i, h: (0, h)),
                  pl.BlockSpec((None, 2, N_KEYS, PEER_HALF), lambda i, h: (h, 0, 0, 0))],
        out_specs=[mapspec] * 4,
        out_shape=mapshapes,
        scratch_shapes=[pltpu.VMEM((2, ng * KEY_PITCH, LANES), F32),
                        pltpu.VMEM((n_lists, PEER_TOPK, ng, LANES), F32),
                        pltpu.VMEM((n_lists, PEER_TOPK, ng, LANES), jnp.int32),
                        pltpu.VMEM((2, PEER_TOPK, ng, LANES), F32),
                        pltpu.VMEM((2, PEER_TOPK, ng, LANES), jnp.int32),
                        pltpu.VMEM((ng * KEY_PITCH, LANES), F32)],
        compiler_params=pltpu.CompilerParams(dimension_semantics=("arbitrary",) * 2,
                                             vmem_limit_bytes=VMEM_LIMIT_BYTES),
        name="peer_route",
    )(xn, wq_bf, keys)


def _gelu_x2(x):
    return x * (1.0 + lax.erf(x * (2.0 ** -0.5)))


BF16_ROWS = 2 * SUBLANES


def _peer_dense_kernel(xnt_ref, u_ref, vt_ref, r2_ref, e2_ref, n1_ref, e1_ref, x1_ref, o_ref,
                       acc_ref, act_ref, w_ref, *, et, tc):
    e = pl.program_id(1)
    ne = pl.num_programs(1)
    a_per = et // N_KEYS
    tt = xnt_ref.shape[1]
    n_chunks = tt // tc
    packed = (N_KEYS // BF16_ROWS, BF16_ROWS, tc)

    @pl.when(e == 0)
    def _():
        acc_ref[...] = jnp.zeros_like(acc_ref)

    def first_key_row(ref, h, arow, cols):
        row = ref[h, pl.ds(arow, 1), cols]
        return jnp.broadcast_to(row, (BF16_ROWS, tc)).astype(BF16)[None]

    def pre_activation(c):
        return _dot(u_ref[...], xnt_ref[:, c * tc:(c + 1) * tc])

    act_ref[0] = pre_activation(0)
    for c in range(n_chunks):
        cols = slice(c * tc, (c + 1) * tc)
        if c + 1 < n_chunks:
            act_ref[(c + 1) % 2] = pre_activation(c + 1)
        for al in range(a_per):
            arow = e * a_per + al
            rows = slice(al * N_KEYS, (al + 1) * N_KEYS)
            gate = jnp.zeros(packed, BF16)
            for h in range(PEER_HEADS):
                n1 = first_key_row(n1_ref, h, arow, cols)
                e1 = first_key_row(e1_ref, h, arow, cols)
                r2 = r2_ref[h, :, cols].reshape(packed)
                e2 = e2_ref[h, :, cols].reshape(packed)
                gate = gate + jnp.where(r2 < n1, e2, jnp.zeros((), BF16)) * e1
            g = _gelu_x2(act_ref[c % 2, rows, :]).astype(BF16)
            w_ref[c, rows, :] = g * gate.reshape(N_KEYS, tc)
        acc_ref[:, cols] += _dot(vt_ref[...], w_ref[c])

    @pl.when(e == ne - 1)
    def _():
        o_ref[...] = x1_ref[...] + acc_ref[...].T


def _peer_dense(xnt, u_bf, vt_bf, maps, x1, *, tt, et, tc):
    d, n = xnt.shape
    n_exp = u_bf.shape[0]
    once = pl.Buffered(1)
    mapspec = pl.BlockSpec((PEER_HEADS, N_KEYS, tt), lambda i, e: (0, 0, i), pipeline_mode=once)
    return pl.pallas_call(
        functools.partial(_peer_dense_kernel, et=et, tc=tc),
        grid=(n // tt, n_exp // et),
        in_specs=[pl.BlockSpec((d, tt), lambda i, e: (0, i), pipeline_mode=once),
                  pl.BlockSpec((et, d), lambda i, e: (e, 0)),
                  pl.BlockSpec((d, et), lambda i, e: (0, e)),
                  mapspec, mapspec, mapspec, mapspec,
                  pl.BlockSpec((tt, d), lambda i, e: (i, 0), pipeline_mode=once)],
        out_specs=pl.BlockSpec((tt, d), lambda i, e: (i, 0)),
        out_shape=jax.ShapeDtypeStruct((n, d), F32),
        scratch_shapes=[pltpu.VMEM((d, tt), F32), pltpu.VMEM((2, et, tc), F32),
                        pltpu.VMEM((tt // tc, et, tc), BF16)],
        compiler_params=pltpu.CompilerParams(dimension_semantics=("arbitrary",) * 2,
                                             vmem_limit_bytes=VMEM_LIMIT_BYTES),
        name="peer_dense",
    )(xnt, u_bf, vt_bf, *maps, x1)


def _tiles(n_tokens, seq):
    tm = min(512, seq)
    tq = min(512, seq)
    tr = SUBLANES * LANES
    tt = min(1024, n_tokens)
    et = 1024
    tc = 256
    return tm, tq, tr, tt, et, tc


def _layer(x2d, batch, seq, lam_init, p):
    n, d = x2d.shape
    tm, tq, tr, tt, et, tc = _tiles(n, seq)
    row = lambda v: v.reshape(1, -1)
    q, k, vt, zc = _in_proj(x2d, row(p["norm1_g"]), p["w_in"].astype(BF16),
                            row(jnp.tile(p["q_norm_g"], DA_WIDTH // DA_HEAD)),
                            row(jnp.tile(p["k_norm_g"], DA_WIDTH // DA_HEAD)), tm=tm)
    lam_vecs = jnp.stack([p["lam_q1"], p["lam_k1"], p["lam_q2"], p["lam_k2"]])
    shp = (batch, seq, DA_WIDTH)
    a = _attention(q.reshape(shp), k.reshape(shp), vt, lam_vecs, p["subln_g"].reshape(-1, 1),
                   lam_init=lam_init, tq=tq).reshape(n, DA_WIDTH)
    x1, xn, xnt = _out_proj(a, zc, x2d, p["w_out"].astype(BF16), p["cf_dw"], row(p["cf_dw_b"]),
                            row(p["cf_ln_g"]), row(p["cf_ln_b"]), p["sc_w"], row(p["norm2_g"]), tm=tm, seq=seq)
    maps = _peer_route(xn, p["peer_wq"].astype(BF16), p["peer_keys"], tr=tr)
    half_vt = (0.5 * p["peer_v"]).astype(BF16).T
    return _peer_dense(xnt, p["peer_u"].astype(BF16), half_vt, maps, x1,
                       tt=tt, et=et, tc=tc)


def kernel(x, norm1_g, w_in, q_norm_g, k_norm_g, lam_q1, lam_k1, lam_q2, lam_k2, subln_g, cf_dw, cf_dw_b,
           cf_ln_g, cf_ln_b, sc_w, w_out, norm2_g, peer_wq, peer_keys, peer_u, peer_v):
    params = dict(norm1_g=norm1_g, w_in=w_in, q_norm_g=q_norm_g, k_norm_g=k_norm_g, lam_q1=lam_q1,
                  lam_k1=lam_k1, lam_q2=lam_q2, lam_k2=lam_k2, subln_g=subln_g, cf_dw=cf_dw,
                  cf_dw_b=cf_dw_b, cf_ln_g=cf_ln_g, cf_ln_b=cf_ln_b, sc_w=sc_w, w_out=w_out,
                  norm2_g=norm2_g, peer_wq=peer_wq, peer_keys=peer_keys, peer_u=peer_u, peer_v=peer_v)
    batch, seq, d = x.shape
    x2d = x.reshape(batch * seq, d)
    for l in range(DEPTH):
        lam_init = 0.8 - 0.6 * math.exp(-0.3 * l)
        x2d = _layer(x2d, batch, seq, lam_init, {name: val[l] for name, val in params.items()})
    return x2d.reshape(batch, seq, d)
```

```python
import functools
import math

import jax
import jax.numpy as jnp
from jax import lax
from jax.experimental import pallas as pl
from jax.experimental.pallas import tpu as pltpu

F32 = jnp.float32
BF16 = jnp.bfloat16

DEPTH = 2
RMS_EPS = 1e-6
LN_EPS = 1e-5
LOG2_E = math.log2(math.e)
DA_HEAD = 64
DA_VDIM = 128
DA_HEADS = 4
DA_WIDTH = 512
CF_WIDTH = 256
CF_GROUPS = 4
CF_CONV = 31
SC_WIDTH = 256
SC_CONV = 3
PEER_HEADS = 8
N_KEYS = 128
PEER_TOPK = 16
PEER_HALF = 128
KEY_PITCH = N_KEYS + 8

LANES = 128
SUBLANES = 8
VMEM_LIMIT_BYTES = 56 * 1024 * 1024

HALO = 32

NT_DIMS = (((1,), (1,)), ((), ()))


def _dot(a, b):
    return jnp.dot(a, b, preferred_element_type=F32)


def _dot_nt(a, b):
    return lax.dot_general(a, b, NT_DIMS, preferred_element_type=F32)


def _group_sum(v, ones_bd):
    hi = v.astype(BF16)
    lo = (v - hi.astype(F32)).astype(BF16)
    return _dot(hi, ones_bd) + _dot(lo, ones_bd)


def _block_diag_ones(width, group):
    r = jnp.arange(width) // group
    return (r[:, None] == r[None, :]).astype(BF16)


def _in_proj_kernel(x_ref, g1_ref, w_ref, qg_ref, kg_ref, ones_ref, q_ref, k_ref, vt_ref, zc_ref):
    x = x_ref[...]
    xn = x * lax.rsqrt(jnp.mean(x * x, axis=-1, keepdims=True) + RMS_EPS) * g1_ref[...]
    z = _dot(xn.astype(BF16), w_ref[...])
    ones_bd = ones_ref[...]

    def qk_norm(t, g):
        ms = _group_sum(t * t, ones_bd) * (1.0 / DA_HEAD)
        return t * lax.rsqrt(ms + RMS_EPS) * g

    o1, o2, o3 = DA_WIDTH, 2 * DA_WIDTH, 3 * DA_WIDTH
    q_ref[...] = (qk_norm(z[:, :o1], qg_ref[...]) * (DA_HEAD ** -0.5 * LOG2_E)).astype(BF16)
    k_ref[...] = qk_norm(z[:, o1:o2], kg_ref[...]).astype(BF16)
    vt_ref[...] = z[:, o2:o3].T.astype(BF16)
    zc_ref[...] = z[:, o3:]


def _in_proj(x, g1, w_in_bf, qg, kg, *, tm):
    n, d = x.shape
    cols = w_in_bf.shape[1]
    zc_w = cols - 3 * DA_WIDTH
    ones_bd = _block_diag_ones(DA_WIDTH, DA_HEAD)
    full = lambda shape: pl.BlockSpec(shape, lambda i: (0,) * len(shape))
    return pl.pallas_call(
        _in_proj_kernel,
        grid=(n // tm,),
        in_specs=[pl.BlockSpec((tm, d), lambda i: (i, 0)), full((1, d)), full((d, cols)),
                  full((1, DA_WIDTH)), full((1, DA_WIDTH)), full((DA_WIDTH, DA_WIDTH))],
        out_specs=[pl.BlockSpec((tm, DA_WIDTH), lambda i: (i, 0))] * 2
        + [pl.BlockSpec((DA_WIDTH, tm), lambda i: (0, i)), pl.BlockSpec((tm, zc_w), lambda i: (i, 0))],
        out_shape=[jax.ShapeDtypeStruct((n, DA_WIDTH), BF16)] * 2
        + [jax.ShapeDtypeStruct((DA_WIDTH, n), BF16), jax.ShapeDtypeStruct((n, zc_w), F32)],
        compiler_params=pltpu.CompilerParams(dimension_semantics=("arbitrary",),
                                             vmem_limit_bytes=VMEM_LIMIT_BYTES),
        name="in_proj",
    )(x, g1, w_in_bf, qg, kg, ones_bd)


NEG_BIG = -1e30


def _attn_kernel(lam_ref, sg_ref, q_ref, k_ref, vt_ref, o_ref, acc_ref, m_ref, l_ref, *, lam_init, tq):
    qi = pl.program_id(2)
    lv = lam_ref[...]
    lam = (jnp.exp(jnp.sum(lv[0:1] * lv[1:2], axis=-1, keepdims=True))
           - jnp.exp(jnp.sum(lv[2:3] * lv[3:4], axis=-1, keepdims=True)) + lam_init)

    q = q_ref[...]
    lane = lax.broadcasted_iota(jnp.int32, q.shape, 1)
    zero = jnp.zeros_like(q)
    qs = (jnp.where(lane < DA_HEAD, q, zero), jnp.where(lane >= DA_HEAD, q, zero))

    acc_ref[...] = jnp.zeros_like(acc_ref)
    m_ref[...] = jnp.full_like(m_ref, NEG_BIG)
    l_ref[...] = jnp.zeros_like(l_ref)

    def step(j, masked):
        start = pl.multiple_of(j * tq, tq)
        vtb = vt_ref[:, pl.ds(start, tq)]

        def scores(c):
            s = _dot_nt(k_ref[pl.ds(start, tq), :], qs[c])
            if masked:
                key = lax.broadcasted_iota(jnp.int32, s.shape, 0)
                qry = lax.broadcasted_iota(jnp.int32, s.shape, 1)
                s = jnp.where(key <= qry, s, NEG_BIG)
            return s

        for c in range(2):
            m_old = m_ref[c]
            m_new = jnp.maximum(m_old, jnp.max(scores(c), axis=0, keepdims=True))
            m_ref[c] = m_new
            alpha = jnp.exp2(m_old - m_new)
            p = jnp.exp2(scores(c) - m_new)
            l_ref[c] = alpha * l_ref[c] + jnp.sum(p, axis=0, keepdims=True)
            acc_ref[c] = alpha * acc_ref[c] + _dot(vtb, p.astype(BF16))

    def body(j, carry):
        step(j, False)
        return carry

    lax.fori_loop(0, qi, body, 0)
    step(qi, True)

    o = acc_ref[0] / l_ref[0] - lam * (acc_ref[1] / l_ref[1])
    o = o * lax.rsqrt(jnp.mean(o * o, axis=0, keepdims=True) + RMS_EPS) * sg_ref[...]
    o_ref[...] = (o * (1.0 - lam_init)).T.astype(BF16)


def _attention(q, k, vt, lam_vecs, subln_g, *, lam_init, tq):
    b, s, _ = q.shape
    qspec = pl.BlockSpec((None, tq, DA_VDIM), lambda bi, h, qi: (bi, qi, h))
    return pl.pallas_call(
        functools.partial(_attn_kernel, lam_init=lam_init, tq=tq),
        grid=(b, DA_HEADS, s // tq),
        in_specs=[pl.BlockSpec((4, DA_HEAD), lambda bi, h, qi: (0, 0)),
                  pl.BlockSpec((DA_VDIM, 1), lambda bi, h, qi: (0, 0)),
                  qspec,
                  pl.BlockSpec((None, s, DA_VDIM), lambda bi, h, qi: (bi, 0, h)),
                  pl.BlockSpec((DA_VDIM, s), lambda bi, h, qi: (h, bi))],
        out_specs=qspec,
        out_shape=jax.ShapeDtypeStruct(q.shape, BF16),
        scratch_shapes=[pltpu.VMEM((2, DA_VDIM, tq), F32), pltpu.VMEM((2, 1, tq), F32),
                        pltpu.VMEM((2, 1, tq), F32)],
        compiler_params=pltpu.CompilerParams(dimension_semantics=("arbitrary",) * 3,
                                             vmem_limit_bytes=VMEM_LIMIT_BYTES),
        name="diff_attention",
    )(lam_vecs, subln_g, q, k, vt)


def _out_proj_kernel(a_ref, zc_ref, halo_ref, x_ref, w_ref, dw_ref, dwb_ref, lng_ref, lnb_ref, scw_ref,
                     g2_ref, ones_ref, x1_ref, xn_ref, xnt_ref, ubuf, pbuf, *, tm, tiles_per_seq):
    i = pl.program_id(0)
    keep = jnp.where(i % tiles_per_seq == 0, 0.0, 1.0)
    c1, c2, c3, c4 = CF_WIDTH, 2 * CF_WIDTH, 2 * CF_WIDTH + SC_WIDTH, 2 * CF_WIDTH + 2 * SC_WIDTH

    def glu(z):
        return z[:, :c1] * jax.nn.sigmoid(z[:, c1:c2])

    def cx(z):
        return z[:, c3:c4] * z[:, c4:]

    zc = zc_ref[...]
    halo = halo_ref[...]
    ubuf[0, 0:HALO, :] = glu(halo) * keep
    ubuf[0, HALO:, :] = glu(zc)
    pbuf[0:HALO, :] = cx(halo) * keep
    pbuf[HALO:, :] = cx(zc)
    shifted_rows = tm + HALO - SUBLANES
    for r in range(1, SUBLANES):
        ubuf[r, 0:shifted_rows, :] = ubuf[0, pl.ds(r, shifted_rows), :]

    dw = dw_ref[...]
    u = jnp.zeros((tm, CF_WIDTH), F32) + dwb_ref[...]
    for j in range(CF_CONV):
        off = HALO - (CF_CONV - 1) + j
        u = u + dw[j:j + 1, :] * ubuf[off % SUBLANES, pl.ds(off - off % SUBLANES, tm), :]
    ones_bd = ones_ref[...]
    gsz = CF_WIDTH // CF_GROUPS
    mu = _group_sum(u, ones_bd) * (1.0 / gsz)
    uc = u - mu
    var = _group_sum(uc * uc, ones_bd) * (1.0 / gsz)
    un = uc * lax.rsqrt(var + LN_EPS) * lng_ref[...] + lnb_ref[...]
    un = un * jax.nn.sigmoid(un)

    scw = scw_ref[...]
    cc = jnp.zeros((tm, SC_WIDTH), F32)
    for j in range(SC_CONV):
        cc = cc + scw[j:j + 1, :] * pbuf[pl.ds(HALO - (SC_CONV - 1) + j, tm), :]
    cc = zc[:, c2:c3] * cc

    y = (_dot(a_ref[...], w_ref[0:DA_WIDTH, :])
         + _dot(un.astype(BF16), w_ref[DA_WIDTH:DA_WIDTH + CF_WIDTH, :])
         + _dot(cc.astype(BF16), w_ref[DA_WIDTH + CF_WIDTH:, :]))
    x1 = x_ref[...] + y
    x1_ref[...] = x1
    xn = x1 * lax.rsqrt(jnp.mean(x1 * x1, axis=-1, keepdims=True) + RMS_EPS) * g2_ref[...]
    xn_ref[...] = xn.astype(BF16)
    xnt_ref[...] = xn.T.astype(BF16)


def _out_proj(a, zc, x, w_out_bf, cf_dw, cf_dw_b, cf_ln_g, cf_ln_b, sc_w, g2, *, tm, seq):
    n, d = x.shape
    zc_w = zc.shape[1]
    hb = tm // HALO
    ones_bd = _block_diag_ones(CF_WIDTH, CF_WIDTH // CF_GROUPS)
    full = lambda shape: pl.BlockSpec(shape, lambda i: (0,) * len(shape))
    row = lambda w: pl.BlockSpec((tm, w), lambda i: (i, 0))
    return pl.pallas_call(
        functools.partial(_out_proj_kernel, tm=tm, tiles_per_seq=seq // tm),
        grid=(n // tm,),
        in_specs=[row(DA_WIDTH), row(zc_w),
                  pl.BlockSpec((HALO, zc_w), lambda i: (jnp.maximum(i * hb - 1, 0), 0)),
                  row(d), full((d, d)), full((CF_CONV, CF_WIDTH)), full((1, CF_WIDTH)),
                  full((1, CF_WIDTH)), full((1, CF_WIDTH)), full((SC_CONV, SC_WIDTH)), full((1, d)),
                  full((CF_WIDTH, CF_WIDTH))],
        out_specs=[row(d), row(d), pl.BlockSpec((d, tm), lambda i: (0, i))],
        out_shape=[jax.ShapeDtypeStruct((n, d), F32), jax.ShapeDtypeStruct((n, d), BF16),
                   jax.ShapeDtypeStruct((d, n), BF16)],
        scratch_shapes=[pltpu.VMEM((SUBLANES, tm + HALO, CF_WIDTH), F32), pltpu.VMEM((tm + HALO, SC_WIDTH), F32)],
        compiler_params=pltpu.CompilerParams(dimension_semantics=("arbitrary",),
                                             vmem_limit_bytes=VMEM_LIMIT_BYTES),
        name="out_proj",
    )(a, zc, zc, x, w_out_bf, cf_dw, cf_dw_b, cf_ln_g, cf_ln_b, sc_w, g2, ones_bd)


def _oddeven_merge_sort_pairs(n):
    pairs = []
    p = 1
    while p < n:
        k = p
        while k >= 1:
            for j in range(k % p, n - k, 2 * k):
                for i in range(min(k, n - j - k)):
                    if (i + j) // (2 * p) == (i + j + k) // (2 * p):
                        pairs.append((i + j, i + j + k))
            k //= 2
        p *= 2
    return pairs


def _bitonic_sort_pairs(n):
    pairs = []
    k = n // 2
    while k >= 1:
        for i in range(n):
            if (i // k) % 2 == 0:
                pairs.append((i, i + k))
        k //= 2
    return pairs


SORT16_PAIRS = _oddeven_merge_sort_pairs(PEER_TOPK)
BITONIC16_PAIRS = _bitonic_sort_pairs(PEER_TOPK)
CAND_CELLS = [(j, k) for j in range(PEER_TOPK) for k in range(PEER_TOPK) if (j + 1) * (k + 1) <= PEER_TOPK]


def _beats(a, b):
    return (a[0] > b[0]) | ((a[0] == b[0]) & (a[1] < b[1]))


def _cmp_exchange(items, i, j, exact):
    a, b = items[i], items[j]
    if exact:
        sw = _beats(b, a)
        items[i] = (jnp.where(sw, b[0], a[0]), jnp.where(sw, b[1], a[1]))
        items[j] = (jnp.where(sw, a[0], b[0]), jnp.where(sw, a[1], b[1]))
    else:
        sw = b[0] > a[0]
        items[i] = (jnp.maximum(a[0], b[0]), jnp.where(sw, b[1], a[1]))
        items[j] = (jnp.minimum(a[0], b[0]), jnp.where(sw, a[1], b[1]))


def _merge_top(a, b, exact):
    out = []
    for i in range(PEER_TOPK):
        x, y = a[i], b[PEER_TOPK - 1 - i]
        if exact:
            sw = _beats(y, x)
            out.append((jnp.where(sw, y[0], x[0]), jnp.where(sw, y[1], x[1])))
        else:
            out.append((jnp.maximum(x[0], y[0]), jnp.where(y[0] > x[0], y[1], x[1])))
    for (i, j) in BITONIC16_PAIRS:
        _cmp_exchange(out, i, j, exact)
    return out


def _peer_route_kernel(xn_ref, wq_ref, keys_ref, r2_ref, e2_ref, n1_ref, e1_ref,
                       sbuf, sval, sidx, topv, topi, obuf, *, tr):
    ng = tr // LANES
    assert ng == SUBLANES
    q = _dot(xn_ref[...], wq_ref[...]).astype(BF16)
    keys = keys_ref[...].astype(BF16)
    for c in range(2):
        st = _dot_nt(keys[c], q[:, c * PEER_HALF:(c + 1) * PEER_HALF])
        for g in range(ng):
            sbuf[c, g * KEY_PITCH:g * KEY_PITCH + N_KEYS, :] = st[:, g * LANES:(g + 1) * LANES]

    def key_row(c, n):
        return sbuf[c, pl.ds(n, ng, stride=KEY_PITCH), :]

    n_groups = N_KEYS // PEER_TOPK

    def top_lists(exact):
        def sort_group(t, carry):
            c = t // n_groups
            grp = t % n_groups
            base = grp * PEER_TOPK
            items = []
            for r in range(PEER_TOPK):
                val = key_row(c, base + r)
                idx = jnp.full(val.shape, r, jnp.int32) + base
                items.append((val, idx))
            for (i, j) in SORT16_PAIRS:
                _cmp_exchange(items, i, j, exact)
            for r in range(PEER_TOPK):
                sval[t, r] = items[r][0]
                sidx[t, r] = items[r][1]
            return carry

        lax.fori_loop(0, 2 * n_groups, sort_group, 0)

        for c in range(2):
            lists = [[(sval[c * n_groups + g, r], sidx[c * n_groups + g, r]) for r in range(PEER_TOPK)]
                     for g in range(n_groups)]
            while len(lists) > 1:
                lists = [_merge_top(lists[2 * m], lists[2 * m + 1], exact) for m in range(len(lists) // 2)]
            for r in range(PEER_TOPK):
                topv[c, r] = lists[0][r][0]
                topi[c, r] = lists[0][r][1]

    top_lists(exact=False)
    doubt = jnp.zeros((ng, LANES), F32)
    for c in range(2):
        kept = [topv[c, r] for r in range(PEER_TOPK)]
        for r in range(PEER_TOPK - 1):
            doubt = jnp.maximum(doubt, jnp.where(kept[r] == kept[r + 1], 1.0, 0.0))
        at_least = sum(jnp.where(key_row(c, n) >= kept[PEER_TOPK - 1], 1.0, 0.0) for n in range(N_KEYS))
        doubt = jnp.maximum(doubt, jnp.where(at_least != float(PEER_TOPK), 1.0, 0.0))

    @pl.when(jnp.max(doubt) > 0.0)
    def _():
        top_lists(exact=True)

    (v1, i1), (v2, i2) = [([topv[c, r] for r in range(PEER_TOPK)], [topi[c, r] for r in range(PEER_TOPK)])
                          for c in range(2)]

    cand = {cell: v1[cell[0]] + v2[cell[1]] for cell in CAND_CELLS}
    rank = {cell: jnp.zeros(v1[0].shape, F32) for cell in CAND_CELLS}
    for pi, p in enumerate(CAND_CELLS):
        for qcell in CAND_CELLS[pi + 1:]:
            if p[0] <= qcell[0] and p[1] <= qcell[1]:
                rank[qcell] = rank[qcell] + 1.0
            else:
                ge = jnp.where(cand[p] >= cand[qcell], 1.0, 0.0)
                rank[qcell] = rank[qcell] + ge
                rank[p] = rank[p] + (1.0 - ge)
    sel = {cell: jnp.where(rank[cell] < PEER_TOPK, 1.0, 0.0) for cell in CAND_CELLS}
    cnt = [sum(sel[(j, k)] for k in range(PEER_TOPK) if (j, k) in sel) for j in range(PEER_TOPK)]

    ej = [jnp.exp(v1[j] - v1[0]) for j in range(PEER_TOPK)]
    ek = [jnp.exp(v2[k] - v2[0]) for k in range(PEER_TOPK)]
    zsum = sum(sel[(j, k)] * (ej[j] * ek[k]) for (j, k) in CAND_CELLS)
    zinv = 1.0 / zsum

    def emit(out_ref, rows):
        for n in range(N_KEYS):
            obuf[pl.ds(n, ng, stride=KEY_PITCH), :] = rows(n)
        for g in range(ng):
            out_ref[:, g * LANES:(g + 1) * LANES] = obuf[g * KEY_PITCH:g * KEY_PITCH + N_KEYS, :].astype(out_ref.dtype)

    def rank2_row(n):
        acc = jnp.full(v1[0].shape, float(PEER_TOPK), F32)
        for k in range(PEER_TOPK):
            acc = jnp.where(i2[k] == n, float(k), acc)
        return acc

    def count1_row(n):
        acc = jnp.zeros(v1[0].shape, F32)
        for j in range(PEER_TOPK):
            acc = jnp.where(i1[j] == n, cnt[j], acc)
        return acc

    emit(r2_ref, rank2_row)
    emit(n1_ref, count1_row)
    emit(e2_ref, lambda n: jnp.exp(key_row(1, n) - v2[0]))
    emit(e1_ref, lambda n: jnp.exp(key_row(0, n) - v1[0]) * zinv)


def _peer_route(xn, wq_bf, keys, *, tr):
    n, d = xn.shape
    hq = wq_bf.shape[1] // PEER_HEADS
    ng = tr // LANES
    mapspec = pl.BlockSpec((None, N_KEYS, tr), lambda i, h: (h, 0, i))
    mapshapes = [jax.ShapeDtypeStruct((PEER_HEADS, N_KEYS, n), dt) for dt in (BF16, BF16, F32, F32)]
    n_lists = 2 * (N_KEYS // PEER_TOPK)
    return pl.pallas_call(
        functools.partial(_peer_route_kernel, tr=tr),
        grid=(n // tr, PEER_HEADS),
        in_specs=[pl.BlockSpec((tr, d), lambda i, h: (i, 0)),
                  pl.BlockSpec((d, hq), lambda i, h: (0, h)),
                  pl.BlockSpec((None, 2, N_KEYS, PEER_HALF), lambda i, h: (h, 0, 0, 0))],
        out_specs=[mapspec] * 4,
        out_shape=mapshapes,
        scratch_shapes=[pltpu.VMEM((2, ng * KEY_PITCH, LANES), F32),
                        pltpu.VMEM((n_lists, PEER_TOPK, ng, LANES), F32),
                        pltpu.VMEM((n_lists, PEER_TOPK, ng, LANES), jnp.int32),
                        pltpu.VMEM((2, PEER_TOPK, ng, LANES), F32),
                        pltpu.VMEM((2, PEER_TOPK, ng, LANES), jnp.int32),
                        pltpu.VMEM((ng * KEY_PITCH, LANES), F32)],
        compiler_params=pltpu.CompilerParams(dimension_semantics=("arbitrary",) * 2,
                                             vmem_limit_bytes=VMEM_LIMIT_BYTES),
        name="peer_route",
    )(xn, wq_bf, keys)


def _gelu_x2(x):
    return x * (1.0 + lax.erf(x * (2.0 ** -0.5)))


BF16_ROWS = 2 * SUBLANES


def _peer_dense_kernel(xnt_ref, u_ref, vt_ref, r2_ref, e2_ref, n1_ref, e1_ref, x1_ref, o_ref,
                       acc_ref, act_ref, w_ref, *, et, tc):
    e = pl.program_id(1)
    ne = pl.num_programs(1)
    a_per = et // N_KEYS
    tt = xnt_ref.shape[1]
    n_chunks = tt // tc
    packed = (N_KEYS // BF16_ROWS, BF16_ROWS, tc)

    @pl.when(e == 0)
    def _():
        acc_ref[...] = jnp.zeros_like(acc_ref)

    def first_key_row(ref, h, arow, cols):
        row = ref[h, pl.ds(arow, 1), cols]
        return jnp.broadcast_to(row, (BF16_ROWS, tc)).astype(BF16)[None]

    def pre_activation(c):
        return _dot(u_ref[...], xnt_ref[:, c * tc:(c + 1) * tc])

    act_ref[0] = pre_activation(0)
    for c in range(n_chunks):
        cols = slice(c * tc, (c + 1) * tc)
        if c + 1 < n_chunks:
            act_ref[(c + 1) % 2] = pre_activation(c + 1)
        for al in range(a_per):
            arow = e * a_per + al
            rows = slice(al * N_KEYS, (al + 1) * N_KEYS)
            gate = jnp.zeros(packed, BF16)
            for h in range(PEER_HEADS):
                n1 = first_key_row(n1_ref, h, arow, cols)
                e1 = first_key_row(e1_ref, h, arow, cols)
                r2 = r2_ref[h, :, cols].reshape(packed)
                e2 = e2_ref[h, :, cols].reshape(packed)
                gate = gate + jnp.where(r2 < n1, e2, jnp.zeros((), BF16)) * e1
            g = _gelu_x2(act_ref[c % 2, rows, :]).astype(BF16)
            w_ref[c, rows, :] = g * gate.reshape(N_KEYS, tc)
        acc_ref[:, cols] += _dot(vt_ref[...], w_ref[c])

    @pl.when(e == ne - 1)
    def _():
        o_ref[...] = x1_ref[...] + acc_ref[...].T


def _peer_dense(xnt, u_bf, vt_bf, maps, x1, *, tt, et, tc):
    d, n = xnt.shape
    n_exp = u_bf.shape[0]
    once = pl.Buffered(1)
    mapspec = pl.BlockSpec((PEER_HEADS, N_KEYS, tt), lambda i, e: (0, 0, i), pipeline_mode=once)
    return pl.pallas_call(
        functools.partial(_peer_dense_kernel, et=et, tc=tc),
        grid=(n // tt, n_exp // et),
        in_specs=[pl.BlockSpec((d, tt), lambda i, e: (0, i), pipeline_mode=once),
                  pl.BlockSpec((et, d), lambda i, e: (e, 0)),
                  pl.BlockSpec((d, et), lambda i, e: (0, e)),
                  mapspec, mapspec, mapspec, mapspec,
                  pl.BlockSpec((tt, d), lambda i, e: (i, 0), pipeline_mode=once)],
        out_specs=pl.BlockSpec((tt, d), lambda i, e: (i, 0)),
        out_shape=jax.ShapeDtypeStruct((n, d), F32),
        scratch_shapes=[pltpu.VMEM((d, tt), F32), pltpu.VMEM((2, et, tc), F32),
                        pltpu.VMEM((tt // tc, et, tc), BF16)],
        compiler_params=pltpu.CompilerParams(dimension_semantics=("arbitrary",) * 2,
                                             vmem_limit_bytes=VMEM_LIMIT_BYTES),
        name="peer_dense",
    )(xnt, u_bf, vt_bf, *maps, x1)


def _tiles(n_tokens, seq):
    tm = min(512, seq)
    tq = min(512, seq)
    tr = SUBLANES * LANES
    tt = min(1024, n_tokens)
    et = 1024
    tc = 256
    return tm, tq, tr, tt, et, tc


def _layer(x2d, batch, seq, lam_init, p):
    n, d = x2d.shape
    tm, tq, tr, tt, et, tc = _tiles(n, seq)
    row = lambda v: v.reshape(1, -1)
    q, k, vt, zc = _in_proj(x2d, row(p["norm1_g"]), p["w_in"].astype(BF16),
                            row(jnp.tile(p["q_norm_g"], DA_WIDTH // DA_HEAD)),
                            row(jnp.tile(p["k_norm_g"], DA_WIDTH // DA_HEAD)), tm=tm)
    lam_vecs = jnp.stack([p["lam_q1"], p["lam_k1"], p["lam_q2"], p["lam_k2"]])
    shp = (batch, seq, DA_WIDTH)
    a = _attention(q.reshape(shp), k.reshape(shp), vt, lam_vecs, p["subln_g"].reshape(-1, 1),
                   lam_init=lam_init, tq=tq).reshape(n, DA_WIDTH)
    x1, xn, xnt = _out_proj(a, zc, x2d, p["w_out"].astype(BF16), p["cf_dw"], row(p["cf_dw_b"]),
                            row(p["cf_ln_g"]), row(p["cf_ln_b"]), p["sc_w"], row(p["norm2_g"]), tm=tm, seq=seq)
    maps = _peer_route(xn, p["peer_wq"].astype(BF16), p["peer_keys"], tr=tr)
    half_vt = (0.5 * p["peer_v"]).astype(BF16).T
    return _peer_dense(xnt, p["peer_u"].astype(BF16), half_vt, maps, x1,
                       tt=tt, et=et, tc=tc)


def kernel(x, norm1_g, w_in, q_norm_g, k_norm_g, lam_q1, lam_k1, lam_q2, lam_k2, subln_g, cf_dw, cf_dw_b,
           cf_ln_g, cf_ln_b, sc_w, w_out, norm2_g, peer_wq, peer_keys, peer_u, peer_v):
    params = dict(norm1_g=norm1_g, w_in=w_in, q_norm_g=q_norm_g, k_norm_g=k_norm_g, lam_q1=lam_q1,
                  lam_k1=lam_k1, lam_q2=lam_q2, lam_k2=lam_k2, subln_g=subln_g, cf_dw=cf_dw,
                  cf_dw_b=cf_dw_b, cf_ln_g=cf_ln_g, cf_ln_b=cf_ln_b, sc_w=sc_w, w_out=w_out,
                  norm2_g=norm2_g, peer_wq=peer_wq, peer_keys=peer_keys, peer_u=peer_u, peer_v=peer_v)
    batch, seq, d = x.shape
    x2d = x.reshape(batch * seq, d)
    for l in range(DEPTH):
        lam_init = 0.8 - 0.6 * math.exp(-0.3 * l)
        x2d = _layer(x2d, batch, seq, lam_init, {name: val[l] for name, val in params.items()})
    return x2d.reshape(batch, seq, d)
```

```python
import functools
import math

import jax
import jax.numpy as jnp
from jax import lax
from jax.experimental import pallas as pl
from jax.experimental.pallas import tpu as pltpu

F32 = jnp.float32
BF16 = jnp.bfloat16

DEPTH = 2
RMS_EPS = 1e-6
LN_EPS = 1e-5
LOG2_E = math.log2(math.e)
DA_HEAD = 64
DA_VDIM = 128
DA_HEADS = 4
DA_WIDTH = 512
CF_WIDTH = 256
CF_GROUPS = 4
CF_CONV = 31
SC_WIDTH = 256
SC_CONV = 3
PEER_HEADS = 8
N_KEYS = 128
PEER_TOPK = 16
PEER_HALF = 128
KEY_PITCH = N_KEYS + 8

LANES = 128
SUBLANES = 8
VMEM_LIMIT_BYTES = 56 * 1024 * 1024

HALO = 32

NT_DIMS = (((1,), (1,)), ((), ()))


def _dot(a, b):
    return jnp.dot(a, b, preferred_element_type=F32)


def _dot_nt(a, b):
    return lax.dot_general(a, b, NT_DIMS, preferred_element_type=F32)


def _group_sum(v, ones_bd):
    hi = v.astype(BF16)
    lo = (v - hi.astype(F32)).astype(BF16)
    return _dot(hi, ones_bd) + _dot(lo, ones_bd)


def _block_diag_ones(width, group):
    r = jnp.arange(width) // group
    return (r[:, None] == r[None, :]).astype(BF16)


def _in_proj_kernel(x_ref, g1_ref, w_ref, qg_ref, kg_ref, ones_ref, q_ref, k_ref, vt_ref, zc_ref):
    x = x_ref[...]
    xn = x * lax.rsqrt(jnp.mean(x * x, axis=-1, keepdims=True) + RMS_EPS) * g1_ref[...]
    z = _dot(xn.astype(BF16), w_ref[...])
    ones_bd = ones_ref[...]

    def qk_norm(t, g):
        ms = _group_sum(t * t, ones_bd) * (1.0 / DA_HEAD)
        return t * lax.rsqrt(ms + RMS_EPS) * g

    o1, o2, o3 = DA_WIDTH, 2 * DA_WIDTH, 3 * DA_WIDTH
    q_ref[...] = (qk_norm(z[:, :o1], qg_ref[...]) * (DA_HEAD ** -0.5 * LOG2_E)).astype(BF16)
    k_ref[...] = qk_norm(z[:, o1:o2], kg_ref[...]).astype(BF16)
    vt_ref[...] = z[:, o2:o3].T.astype(BF16)
    zc_ref[...] = z[:, o3:]


def _in_proj(x, g1, w_in_bf, qg, kg, *, tm):
    n, d = x.shape
    cols = w_in_bf.shape[1]
    zc_w = cols - 3 * DA_WIDTH
    ones_bd = _block_diag_ones(DA_WIDTH, DA_HEAD)
    full = lambda shape: pl.BlockSpec(shape, lambda i: (0,) * len(shape))
    return pl.pallas_call(
        _in_proj_kernel,
        grid=(n // tm,),
        in_specs=[pl.BlockSpec((tm, d), lambda i: (i, 0)), full((1, d)), full((d, cols)),
                  full((1, DA_WIDTH)), full((1, DA_WIDTH)), full((DA_WIDTH, DA_WIDTH))],
        out_specs=[pl.BlockSpec((tm, DA_WIDTH), lambda i: (i, 0))] * 2
        + [pl.BlockSpec((DA_WIDTH, tm), lambda i: (0, i)), pl.BlockSpec((tm, zc_w), lambda i: (i, 0))],
        out_shape=[jax.ShapeDtypeStruct((n, DA_WIDTH), BF16)] * 2
        + [jax.ShapeDtypeStruct((DA_WIDTH, n), BF16), jax.ShapeDtypeStruct((n, zc_w), F32)],
        compiler_params=pltpu.CompilerParams(dimension_semantics=("arbitrary",),
                                             vmem_limit_bytes=VMEM_LIMIT_BYTES),
        name="in_proj",
    )(x, g1, w_in_bf, qg, kg, ones_bd)


NEG_BIG = -1e30


def _attn_kernel(lam_ref, sg_ref, q_ref, k_ref, vt_ref, o_ref, acc_ref, m_ref, l_ref, *, lam_init, tq):
    qi = pl.program_id(2)
    lv = lam_ref[...]
    lam = (jnp.exp(jnp.sum(lv[0:1] * lv[1:2], axis=-1, keepdims=True))
           - jnp.exp(jnp.sum(lv[2:3] * lv[3:4], axis=-1, keepdims=True)) + lam_init)

    q = q_ref[...]
    lane = lax.broadcasted_iota(jnp.int32, q.shape, 1)
    zero = jnp.zeros_like(q)
    qs = (jnp.where(lane < DA_HEAD, q, zero), jnp.where(lane >= DA_HEAD, q, zero))

    acc_ref[...] = jnp.zeros_like(acc_ref)
    m_ref[...] = jnp.full_like(m_ref, NEG_BIG)
    l_ref[...] = jnp.zeros_like(l_ref)

    def step(j, masked):
        start = pl.multiple_of(j * tq, tq)
        vtb = vt_ref[:, pl.ds(start, tq)]

        def scores(c):
            s = _dot_nt(k_ref[pl.ds(start, tq), :], qs[c])
            if masked:
                key = lax.broadcasted_iota(jnp.int32, s.shape, 0)
                qry = lax.broadcasted_iota(jnp.int32, s.shape, 1)
                s = jnp.where(key <= qry, s, NEG_BIG)
            return s

        for c in range(2):
            m_old = m_ref[c]
            m_new = jnp.maximum(m_old, jnp.max(scores(c), axis=0, keepdims=True))
            m_ref[c] = m_new
            alpha = jnp.exp2(m_old - m_new)
            p = jnp.exp2(scores(c) - m_new)
            l_ref[c] = alpha * l_ref[c] + jnp.sum(p, axis=0, keepdims=True)
            acc_ref[c] = alpha * acc_ref[c] + _dot(vtb, p.astype(BF16))

    def body(j, carry):
        step(j, False)
        return carry

    lax.fori_loop(0, qi, body, 0)
    step(qi, True)

    o = acc_ref[0] / l_ref[0] - lam * (acc_ref[1] / l_ref[1])
    o = o * lax.rsqrt(jnp.mean(o * o, axis=0, keepdims=True) + RMS_EPS) * sg_ref[...]
    o_ref[...] = (o * (1.0 - lam_init)).T.astype(BF16)


def _attention(q, k, vt, lam_vecs, subln_g, *, lam_init, tq):
    b, s, _ = q.shape
    qspec = pl.BlockSpec((None, tq, DA_VDIM), lambda bi, h, qi: (bi, qi, h))
    return pl.pallas_call(
        functools.partial(_attn_kernel, lam_init=lam_init, tq=tq),
        grid=(b, DA_HEADS, s // tq),
        in_specs=[pl.BlockSpec((4, DA_HEAD), lambda bi, h, qi: (0, 0)),
                  pl.BlockSpec((DA_VDIM, 1), lambda bi, h, qi: (0, 0)),
                  qspec,
                  pl.BlockSpec((None, s, DA_VDIM), lambda bi, h, qi: (bi, 0, h)),
                  pl.BlockSpec((DA_VDIM, s), lambda bi, h, qi: (h, bi))],
        out_specs=qspec,
        out_shape=jax.ShapeDtypeStruct(q.shape, BF16),
        scratch_shapes=[pltpu.VMEM((2, DA_VDIM, tq), F32), pltpu.VMEM((2, 1, tq), F32),
                        pltpu.VMEM((2, 1, tq), F32)],
        compiler_params=pltpu.CompilerParams(dimension_semantics=("arbitrary",) * 3,
                                             vmem_limit_bytes=VMEM_LIMIT_BYTES),
        name="diff_attention",
    )(lam_vecs, subln_g, q, k, vt)


def _out_proj_kernel(a_ref, zc_ref, halo_ref, x_ref, w_ref, dw_ref, dwb_ref, lng_ref, lnb_ref, scw_ref,
                     g2_ref, ones_ref, x1_ref, xn_ref, xnt_ref, ubuf, pbuf, *, tm, tiles_per_seq):
    i = pl.program_id(0)
    keep = jnp.where(i % tiles_per_seq == 0, 0.0, 1.0)
    c1, c2, c3, c4 = CF_WIDTH, 2 * CF_WIDTH, 2 * CF_WIDTH + SC_WIDTH, 2 * CF_WIDTH + 2 * SC_WIDTH

    def glu(z):
        return z[:, :c1] * jax.nn.sigmoid(z[:, c1:c2])

    def cx(z):
        return z[:, c3:c4] * z[:, c4:]

    zc = zc_ref[...]
    halo = halo_ref[...]
    ubuf[0, 0:HALO, :] = glu(halo) * keep
    ubuf[0, HALO:, :] = glu(zc)
    pbuf[0:HALO, :] = cx(halo) * keep
    pbuf[HALO:, :] = cx(zc)
    shifted_rows = tm + HALO - SUBLANES
    for r in range(1, SUBLANES):
        ubuf[r, 0:shifted_rows, :] = ubuf[0, pl.ds(r, shifted_rows), :]

    dw = dw_ref[...]
    u = jnp.zeros((tm, CF_WIDTH), F32) + dwb_ref[...]
    for j in range(CF_CONV):
        off = HALO - (CF_CONV - 1) + j
        u = u + dw[j:j + 1, :] * ubuf[off % SUBLANES, pl.ds(off - off % SUBLANES, tm), :]
    ones_bd = ones_ref[...]
    gsz = CF_WIDTH // CF_GROUPS
    mu = _group_sum(u, ones_bd) * (1.0 / gsz)
    uc = u - mu
    var = _group_sum(uc * uc, ones_bd) * (1.0 / gsz)
    un = uc * lax.rsqrt(var + LN_EPS) * lng_ref[...] + lnb_ref[...]
    un = un * jax.nn.sigmoid(un)

    scw = scw_ref[...]
    cc = jnp.zeros((tm, SC_WIDTH), F32)
    for j in range(SC_CONV):
        cc = cc + scw[j:j + 1, :] * pbuf[pl.ds(HALO - (SC_CONV - 1) + j, tm), :]
    cc = zc[:, c2:c3] * cc

    y = (_dot(a_ref[...], w_ref[0:DA_WIDTH, :])
         + _dot(un.astype(BF16), w_ref[DA_WIDTH:DA_WIDTH + CF_WIDTH, :])
         + _dot(cc.astype(BF16), w_ref[DA_WIDTH + CF_WIDTH:, :]))
    x1 = x_ref[...] + y
    x1_ref[...] = x1
    xn = x1 * lax.rsqrt(jnp.mean(x1 * x1, axis=-1, keepdims=True) + RMS_EPS) * g2_ref[...]
    xn_ref[...] = xn.astype(BF16)
    xnt_ref[...] = xn.T.astype(BF16)


def _out_proj(a, zc, x, w_out_bf, cf_dw, cf_dw_b, cf_ln_g, cf_ln_b, sc_w, g2, *, tm, seq):
    n, d = x.shape
    zc_w = zc.shape[1]
    hb = tm // HALO
    ones_bd = _block_diag_ones(CF_WIDTH, CF_WIDTH // CF_GROUPS)
    full = lambda shape: pl.BlockSpec(shape, lambda i: (0,) * len(shape))
    row = lambda w: pl.BlockSpec((tm, w), lambda i: (i, 0))
    return pl.pallas_call(
        functools.partial(_out_proj_kernel, tm=tm, tiles_per_seq=seq // tm),
        grid=(n // tm,),
        in_specs=[row(DA_WIDTH), row(zc_w),
                  pl.BlockSpec((HALO, zc_w), lambda i: (jnp.maximum(i * hb - 1, 0), 0)),
                  row(d), full((d, d)), full((CF_CONV, CF_WIDTH)), full((1, CF_WIDTH)),
                  full((1, CF_WIDTH)), full((1, CF_WIDTH)), full((SC_CONV, SC_WIDTH)), full((1, d)),
                  full((CF_WIDTH, CF_WIDTH))],
        out_specs=[row(d), row(d), pl.BlockSpec((d, tm), lambda i: (0, i))],
        out_shape=[jax.ShapeDtypeStruct((n, d), F32), jax.ShapeDtypeStruct((n, d), BF16),
                   jax.ShapeDtypeStruct((d, n), BF16)],
        scratch_shapes=[pltpu.VMEM((SUBLANES, tm + HALO, CF_WIDTH), F32), pltpu.VMEM((tm + HALO, SC_WIDTH), F32)],
        compiler_params=pltpu.CompilerParams(dimension_semantics=("arbitrary",),
                                             vmem_limit_bytes=VMEM_LIMIT_BYTES),
        name="out_proj",
    )(a, zc, zc, x, w_out_bf, cf_dw, cf_dw_b, cf_ln_g, cf_ln_b, sc_w, g2, ones_bd)


def _oddeven_merge_sort_pairs(n):
    pairs = []
    p = 1
    while p < n:
        k = p
        while k >= 1:
            for j in range(k % p, n - k, 2 * k):
                for i in range(min(k, n - j - k)):
                    if (i + j) // (2 * p) == (i + j + k) // (2 * p):
                        pairs.append((i + j, i + j + k))
            k //= 2
        p *= 2
    return pairs


def _bitonic_sort_pairs(n):
    pairs = []
    k = n // 2
    while k >= 1:
        for i in range(n):
            if (i // k) % 2 == 0:
                pairs.append((i, i + k))
        k //= 2
    return pairs


SORT16_PAIRS = _oddeven_merge_sort_pairs(PEER_TOPK)
BITONIC16_PAIRS = _bitonic_sort_pairs(PEER_TOPK)
CAND_CELLS = [(j, k) for j in range(PEER_TOPK) for k in range(PEER_TOPK) if (j + 1) * (k + 1) <= PEER_TOPK]


def _beats(a, b):
    return (a[0] > b[0]) | ((a[0] == b[0]) & (a[1] < b[1]))


def _cmp_exchange(items, i, j, exact):
    a, b = items[i], items[j]
    if exact:
        sw = _beats(b, a)
        items[i] = (jnp.where(sw, b[0], a[0]), jnp.where(sw, b[1], a[1]))
        items[j] = (jnp.where(sw, a[0], b[0]), jnp.where(sw, a[1], b[1]))
    else:
        sw = b[0] > a[0]
        items[i] = (jnp.maximum(a[0], b[0]), jnp.where(sw, b[1], a[1]))
        items[j] = (jnp.minimum(a[0], b[0]), jnp.where(sw, a[1], b[1]))


def _merge_top(a, b, exact):
    out = []
    for i in range(PEER_TOPK):
        x, y = a[i], b[PEER_TOPK - 1 - i]
        if exact:
            sw = _beats(y, x)
            out.append((jnp.where(sw, y[0], x[0]), jnp.where(sw, y[1], x[1])))
        else:
            out.append((jnp.maximum(x[0], y[0]), jnp.where(y[0] > x[0], y[1], x[1])))
    for (i, j) in BITONIC16_PAIRS:
        _cmp_exchange(out, i, j, exact)
    return out


def _peer_route_kernel(xn_ref, wq_ref, keys_ref, r2_ref, e2_ref, n1_ref, e1_ref,
                       sbuf, sval, sidx, topv, topi, obuf, *, tr):
    ng = tr // LANES
    assert ng == SUBLANES
    q = _dot(xn_ref[...], wq_ref[...]).astype(BF16)
    keys = keys_ref[...].astype(BF16)
    for c in range(2):
        st = _dot_nt(keys[c], q[:, c * PEER_HALF:(c + 1) * PEER_HALF])
        for g in range(ng):
            sbuf[c, g * KEY_PITCH:g * KEY_PITCH + N_KEYS, :] = st[:, g * LANES:(g + 1) * LANES]

    def key_row(c, n):
        return sbuf[c, pl.ds(n, ng, stride=KEY_PITCH), :]

    n_groups = N_KEYS // PEER_TOPK

    def top_lists(exact):
        def sort_group(t, carry):
            c = t // n_groups
            grp = t % n_groups
            base = grp * PEER_TOPK
            items = []
            for r in range(PEER_TOPK):
                val = key_row(c, base + r)
                idx = jnp.full(val.shape, r, jnp.int32) + base
                items.append((val, idx))
            for (i, j) in SORT16_PAIRS:
                _cmp_exchange(items, i, j, exact)
            for r in range(PEER_TOPK):
                sval[t, r] = items[r][0]
                sidx[t, r] = items[r][1]
            return carry

        lax.fori_loop(0, 2 * n_groups, sort_group, 0)

        for c in range(2):
            lists = [[(sval[c * n_groups + g, r], sidx[c * n_groups + g, r]) for r in range(PEER_TOPK)]
                     for g in range(n_groups)]
            while len(lists) > 1:
                lists = [_merge_top(lists[2 * m], lists[2 * m + 1], exact) for m in range(len(lists) // 2)]
            for r in range(PEER_TOPK):
                topv[c, r] = lists[0][r][0]
                topi[c, r] = lists[0][r][1]

    top_lists(exact=False)
    doubt = jnp.zeros((ng, LANES), F32)
    for c in range(2):
        kept = [topv[c, r] for r in range(PEER_TOPK)]
        for r in range(PEER_TOPK - 1):
            doubt = jnp.maximum(doubt, jnp.where(kept[r] == kept[r + 1], 1.0, 0.0))
        at_least = sum(jnp.where(key_row(c, n) >= kept[PEER_TOPK - 1], 1.0, 0.0) for n in range(N_KEYS))
        doubt = jnp.maximum(doubt, jnp.where(at_least != float(PEER_TOPK), 1.0, 0.0))

    @pl.when(jnp.max(doubt) > 0.0)
    def _():
        top_lists(exact=True)

    (v1, i1), (v2, i2) = [([topv[c, r] for r in range(PEER_TOPK)], [topi[c, r] for r in range(PEER_TOPK)])
                          for c in range(2)]

    cand = {cell: v1[cell[0]] + v2[cell[1]] for cell in CAND_CELLS}
    rank = {cell: jnp.zeros(v1[0].shape, F32) for cell in CAND_CELLS}
    for pi, p in enumerate(CAND_CELLS):
        for qcell in CAND_CELLS[pi + 1:]:
            if p[0] <= qcell[0] and p[1] <= qcell[1]:
                rank[qcell] = rank[qcell] + 1.0
            else:
                ge = jnp.where(cand[p] >= cand[qcell], 1.0, 0.0)
                rank[qcell] = rank[qcell] + ge
                rank[p] = rank[p] + (1.0 - ge)
    sel = {cell: jnp.where(rank[cell] < PEER_TOPK, 1.0, 0.0) for cell in CAND_CELLS}
    cnt = [sum(sel[(j, k)] for k in range(PEER_TOPK) if (j, k) in sel) for j in range(PEER_TOPK)]

    ej = [jnp.exp(v1[j] - v1[0]) for j in range(PEER_TOPK)]
    ek = [jnp.exp(v2[k] - v2[0]) for k in range(PEER_TOPK)]
    zsum = sum(sel[(j, k)] * (ej[j] * ek[k]) for (j, k) in CAND_CELLS)
    zinv = 1.0 / zsum

    def emit(out_ref, rows):
        for n in range(N_KEYS):
            obuf[pl.ds(n, ng, stride=KEY_PITCH), :] = rows(n)
        for g in range(ng):
            out_ref[:, g * LANES:(g + 1) * LANES] = obuf[g * KEY_PITCH:g * KEY_PITCH + N_KEYS, :].astype(out_ref.dtype)

    def rank2_row(n):
        acc = jnp.full(v1[0].shape, float(PEER_TOPK), F32)
        for k in range(PEER_TOPK):
            acc = jnp.where(i2[k] == n, float(k), acc)
        return acc

    def count1_row(n):
        acc = jnp.zeros(v1[0].shape, F32)
        for j in range(PEER_TOPK):
            acc = jnp.where(i1[j] == n, cnt[j], acc)
        return acc

    emit(r2_ref, rank2_row)
    emit(n1_ref, count1_row)
    emit(e2_ref, lambda n: jnp.exp(key_row(1, n) - v2[0]))
    emit(e1_ref, lambda n: jnp.exp(key_row(0, n) - v1[0]) * zinv)


def _peer_route(xn, wq_bf, keys, *, tr):
    n, d = xn.shape
    hq = wq_bf.shape[1] // PEER_HEADS
    ng = tr // LANES
    mapspec = pl.BlockSpec((None, N_KEYS, tr), lambda i, h: (h, 0, i))
    mapshapes = [jax.ShapeDtypeStruct((PEER_HEADS, N_KEYS, n), dt) for dt in (BF16, BF16, F32, F32)]
    n_lists = 2 * (N_KEYS // PEER_TOPK)
    return pl.pallas_call(
        functools.partial(_peer_route_kernel, tr=tr),
        grid=(n // tr, PEER_HEADS),
        in_specs=[pl.BlockSpec((tr, d), lambda i, h: (i, 0)),
                  pl.BlockSpec((d, hq), lambda i, h: (0, h)),
                  pl.BlockSpec((None, 2, N_KEYS, PEER_HALF), lambda i, h: (h, 0, 0, 0))],
        out_specs=[mapspec] * 4,
        out_shape=mapshapes,
        scratch_shapes=[pltpu.VMEM((2, ng * KEY_PITCH, LANES), F32),
                        pltpu.VMEM((n_lists, PEER_TOPK, ng, LANES), F32),
                        pltpu.VMEM((n_lists, PEER_TOPK, ng, LANES), jnp.int32),
                        pltpu.VMEM((2, PEER_TOPK, ng, LANES), F32),
                        pltpu.VMEM((2, PEER_TOPK, ng, LANES), jnp.int32),
                        pltpu.VMEM((ng * KEY_PITCH, LANES), F32)],
        compiler_params=pltpu.CompilerParams(dimension_semantics=("arbitrary",) * 2,
                                             vmem_limit_bytes=VMEM_LIMIT_BYTES),
        name="peer_route",
    )(xn, wq_bf, keys)


def _gelu_x2(x):
    return x * (1.0 + lax.erf(x * (2.0 ** -0.5)))


BF16_ROWS = 2 * SUBLANES


def _peer_dense_kernel(xnt_ref, u_ref, vt_ref, r2_ref, e2_ref, n1_ref, e1_ref, x1_ref, o_ref,
                       acc_ref, act_ref, w_ref, *, et, tc):
    e = pl.program_id(1)
    ne = pl.num_programs(1)
    a_per = et // N_KEYS
    tt = xnt_ref.shape[1]
    n_chunks = tt // tc
    packed = (N_KEYS // BF16_ROWS, BF16_ROWS, tc)

    @pl.when(e == 0)
    def _():
        acc_ref[...] = jnp.zeros_like(acc_ref)

    def first_key_row(ref, h, arow, cols):
        row = ref[h, pl.ds(arow, 1), cols]
        return jnp.broadcast_to(row, (BF16_ROWS, tc)).astype(BF16)[None]

    def pre_activation(c):
        return _dot(u_ref[...], xnt_ref[:, c * tc:(c + 1) * tc])

    act_ref[0] = pre_activation(0)
    for c in range(n_chunks):
        cols = slice(c * tc, (c + 1) * tc)
        if c + 1 < n_chunks:
            act_ref[(c + 1) % 2] = pre_activation(c + 1)
        for al in range(a_per):
            arow = e * a_per + al
            rows = slice(al * N_KEYS, (al + 1) * N_KEYS)
            gate = jnp.zeros(packed, BF16)
            for h in range(PEER_HEADS):
                n1 = first_key_row(n1_ref, h, arow, cols)
                e1 = first_key_row(e1_ref, h, arow, cols)
                r2 = r2_ref[h, :, cols].reshape(packed)
                e2 = e2_ref[h, :, cols].reshape(packed)
                gate = gate + jnp.where(r2 < n1, e2, jnp.zeros((), BF16)) * e1
            g = _gelu_x2(act_ref[c % 2, rows, :]).astype(BF16)
            w_ref[c, rows, :] = g * gate.reshape(N_KEYS, tc)
        acc_ref[:, cols] += _dot(vt_ref[...], w_ref[c])

    @pl.when(e == ne - 1)
    def _():
        o_ref[...] = x1_ref[...] + acc_ref[...].T


def _peer_dense(xnt, u_bf, vt_bf, maps, x1, *, tt, et, tc):
    d, n = xnt.shape
    n_exp = u_bf.shape[0]
    once = pl.Buffered(1)
    mapspec = pl.BlockSpec((PEER_HEADS, N_KEYS, tt), lambda i, e: (0, 0, i), pipeline_mode=once)
    mapspec2 = pl.BlockSpec((PEER_HEADS, N_KEYS, tt), lambda i, e: (0, 0, i))
    return pl.pallas_call(
        functools.partial(_peer_dense_kernel, et=et, tc=tc),
        grid=(n // tt, n_exp // et),
        in_specs=[pl.BlockSpec((d, tt), lambda i, e: (0, i), pipeline_mode=once),
                  pl.BlockSpec((et, d), lambda i, e: (e, 0)),
                  pl.BlockSpec((d, et), lambda i, e: (0, e)),
                  mapspec2, mapspec2, mapspec, mapspec,
                  pl.BlockSpec((tt, d), lambda i, e: (i, 0), pipeline_mode=once)],
        out_specs=pl.BlockSpec((tt, d), lambda i, e: (i, 0)),
        out_shape=jax.ShapeDtypeStruct((n, d), F32),
        scratch_shapes=[pltpu.VMEM((d, tt), F32), pltpu.VMEM((2, et, tc), F32),
                        pltpu.VMEM((tt // tc, et, tc), BF16)],
        compiler_params=pltpu.CompilerParams(dimension_semantics=("arbitrary",) * 2,
                                             vmem_limit_bytes=VMEM_LIMIT_BYTES),
        name="peer_dense",
    )(xnt, u_bf, vt_bf, *maps, x1)


def _tiles(n_tokens, seq):
    tm = min(512, seq)
    tq = min(512, seq)
    tr = SUBLANES * LANES
    tt = min(1024, n_tokens)
    et = 1024
    tc = 256
    return tm, tq, tr, tt, et, tc


def _layer(x2d, batch, seq, lam_init, p):
    n, d = x2d.shape
    tm, tq, tr, tt, et, tc = _tiles(n, seq)
    row = lambda v: v.reshape(1, -1)
    q, k, vt, zc = _in_proj(x2d, row(p["norm1_g"]), p["w_in"].astype(BF16),
                            row(jnp.tile(p["q_norm_g"], DA_WIDTH // DA_HEAD)),
                            row(jnp.tile(p["k_norm_g"], DA_WIDTH // DA_HEAD)), tm=tm)
    lam_vecs = jnp.stack([p["lam_q1"], p["lam_k1"], p["lam_q2"], p["lam_k2"]])
    shp = (batch, seq, DA_WIDTH)
    a = _attention(q.reshape(shp), k.reshape(shp), vt, lam_vecs, p["subln_g"].reshape(-1, 1),
                   lam_init=lam_init, tq=tq).reshape(n, DA_WIDTH)
    x1, xn, xnt = _out_proj(a, zc, x2d, p["w_out"].astype(BF16), p["cf_dw"], row(p["cf_dw_b"]),
                            row(p["cf_ln_g"]), row(p["cf_ln_b"]), p["sc_w"], row(p["norm2_g"]), tm=tm, seq=seq)
    maps = _peer_route(xn, p["peer_wq"].astype(BF16), p["peer_keys"], tr=tr)
    half_vt = (0.5 * p["peer_v"]).astype(BF16).T
    return _peer_dense(xnt, p["peer_u"].astype(BF16), half_vt, maps, x1,
                       tt=tt, et=et, tc=tc)


def kernel(x, norm1_g, w_in, q_norm_g, k_norm_g, lam_q1, lam_k1, lam_q2, lam_k2, subln_g, cf_dw, cf_dw_b,
           cf_ln_g, cf_ln_b, sc_w, w_out, norm2_g, peer_wq, peer_keys, peer_u, peer_v):
    params = dict(norm1_g=norm1_g, w_in=w_in, q_norm_g=q_norm_g, k_norm_g=k_norm_g, lam_q1=lam_q1,
                  lam_k1=lam_k1, lam_q2=lam_q2, lam_k2=lam_k2, subln_g=subln_g, cf_dw=cf_dw,
                  cf_dw_b=cf_dw_b, cf_ln_g=cf_ln_g, cf_ln_b=cf_ln_b, sc_w=sc_w, w_out=w_out,
                  norm2_g=norm2_g, peer_wq=peer_wq, peer_keys=peer_keys, peer_u=peer_u, peer_v=peer_v)
    batch, seq, d = x.shape
    x2d = x.reshape(batch * seq, d)
    for l in range(DEPTH):
        lam_init = 0.8 - 0.6 * math.exp(-0.3 * l)
        x2d = _layer(x2d, batch, seq, lam_init, {name: val[l] for name, val in params.items()})
    return x2d.reshape(batch, seq, d)
```

```python
import functools
import math

import jax
import jax.numpy as jnp
from jax import lax
from jax.experimental import pallas as pl
from jax.experimental.pallas import tpu as pltpu

F32 = jnp.float32
BF16 = jnp.bfloat16

DEPTH = 2
RMS_EPS = 1e-6
LN_EPS = 1e-5
LOG2_E = math.log2(math.e)
DA_HEAD = 64
DA_VDIM = 128
DA_HEADS = 4
DA_WIDTH = 512
CF_WIDTH = 256
CF_GROUPS = 4
CF_CONV = 31
SC_WIDTH = 256
SC_CONV = 3
PEER_HEADS = 8
N_KEYS = 128
PEER_TOPK = 16
PEER_HALF = 128
KEY_PITCH = N_KEYS + 8

LANES = 128
SUBLANES = 8
VMEM_LIMIT_BYTES = 56 * 1024 * 1024

HALO = 32

NT_DIMS = (((1,), (1,)), ((), ()))


def _dot(a, b):
    return jnp.dot(a, b, preferred_element_type=F32)


def _dot_nt(a, b):
    return lax.dot_general(a, b, NT_DIMS, preferred_element_type=F32)


def _group_sum(v, ones_bd):
    hi = v.astype(BF16)
    lo = (v - hi.astype(F32)).astype(BF16)
    return _dot(hi, ones_bd) + _dot(lo, ones_bd)


def _block_diag_ones(width, group):
    r = jnp.arange(width) // group
    return (r[:, None] == r[None, :]).astype(BF16)


def _in_proj_kernel(x_ref, g1_ref, w_ref, qg_ref, kg_ref, ones_ref, q_ref, k_ref, vt_ref, zc_ref):
    x = x_ref[...]
    xn = x * lax.rsqrt(jnp.mean(x * x, axis=-1, keepdims=True) + RMS_EPS) * g1_ref[...]
    z = _dot(xn.astype(BF16), w_ref[...])
    ones_bd = ones_ref[...]

    def qk_norm(t, g):
        ms = _group_sum(t * t, ones_bd) * (1.0 / DA_HEAD)
        return t * lax.rsqrt(ms + RMS_EPS) * g

    o1, o2, o3 = DA_WIDTH, 2 * DA_WIDTH, 3 * DA_WIDTH
    q_ref[...] = (qk_norm(z[:, :o1], qg_ref[...]) * (DA_HEAD ** -0.5 * LOG2_E)).astype(BF16)
    k_ref[...] = qk_norm(z[:, o1:o2], kg_ref[...]).astype(BF16)
    vt_ref[...] = z[:, o2:o3].T.astype(BF16)
    zc_ref[...] = z[:, o3:]


def _in_proj(x, g1, w_in_bf, qg, kg, *, tm):
    n, d = x.shape
    cols = w_in_bf.shape[1]
    zc_w = cols - 3 * DA_WIDTH
    ones_bd = _block_diag_ones(DA_WIDTH, DA_HEAD)
    full = lambda shape: pl.BlockSpec(shape, lambda i: (0,) * len(shape))
    return pl.pallas_call(
        _in_proj_kernel,
        grid=(n // tm,),
        in_specs=[pl.BlockSpec((tm, d), lambda i: (i, 0)), full((1, d)), full((d, cols)),
                  full((1, DA_WIDTH)), full((1, DA_WIDTH)), full((DA_WIDTH, DA_WIDTH))],
        out_specs=[pl.BlockSpec((tm, DA_WIDTH), lambda i: (i, 0))] * 2
        + [pl.BlockSpec((DA_WIDTH, tm), lambda i: (0, i)), pl.BlockSpec((tm, zc_w), lambda i: (i, 0))],
        out_shape=[jax.ShapeDtypeStruct((n, DA_WIDTH), BF16)] * 2
        + [jax.ShapeDtypeStruct((DA_WIDTH, n), BF16), jax.ShapeDtypeStruct((n, zc_w), F32)],
        compiler_params=pltpu.CompilerParams(dimension_semantics=("arbitrary",),
                                             vmem_limit_bytes=VMEM_LIMIT_BYTES),
        name="in_proj",
    )(x, g1, w_in_bf, qg, kg, ones_bd)


NEG_BIG = -1e30


def _attn_kernel(lam_ref, sg_ref, q_ref, k_ref, vt_ref, o_ref, acc_ref, m_ref, l_ref, *, lam_init, tq):
    qi = pl.program_id(2)
    lv = lam_ref[...]
    lam = (jnp.exp(jnp.sum(lv[0:1] * lv[1:2], axis=-1, keepdims=True))
           - jnp.exp(jnp.sum(lv[2:3] * lv[3:4], axis=-1, keepdims=True)) + lam_init)

    q = q_ref[...]
    lane = lax.broadcasted_iota(jnp.int32, q.shape, 1)
    zero = jnp.zeros_like(q)
    qs = (jnp.where(lane < DA_HEAD, q, zero), jnp.where(lane >= DA_HEAD, q, zero))

    acc_ref[...] = jnp.zeros_like(acc_ref)
    m_ref[...] = jnp.full_like(m_ref, NEG_BIG)
    l_ref[...] = jnp.zeros_like(l_ref)

    def step(j, masked):
        start = pl.multiple_of(j * tq, tq)
        vtb = vt_ref[:, pl.ds(start, tq)]

        def scores(c):
            s = _dot_nt(k_ref[pl.ds(start, tq), :], qs[c])
            if masked:
                key = lax.broadcasted_iota(jnp.int32, s.shape, 0)
                qry = lax.broadcasted_iota(jnp.int32, s.shape, 1)
                s = jnp.where(key <= qry, s, NEG_BIG)
            return s

        for c in range(2):
            m_old = m_ref[c]
            m_new = jnp.maximum(m_old, jnp.max(scores(c), axis=0, keepdims=True))
            m_ref[c] = m_new
            alpha = jnp.exp2(m_old - m_new)
            p = jnp.exp2(scores(c) - m_new)
            l_ref[c] = alpha * l_ref[c] + jnp.sum(p, axis=0, keepdims=True)
            acc_ref[c] = alpha * acc_ref[c] + _dot(vtb, p.astype(BF16))

    def body(j, carry):
        step(j, False)
        return carry

    lax.fori_loop(0, qi, body, 0)
    step(qi, True)

    o = acc_ref[0] / l_ref[0] - lam * (acc_ref[1] / l_ref[1])
    o = o * lax.rsqrt(jnp.mean(o * o, axis=0, keepdims=True) + RMS_EPS) * sg_ref[...]
    o_ref[...] = (o * (1.0 - lam_init)).T.astype(BF16)


def _attention(q, k, vt, lam_vecs, subln_g, *, lam_init, tq):
    b, s, _ = q.shape
    qspec = pl.BlockSpec((None, tq, DA_VDIM), lambda bi, h, qi: (bi, qi, h))
    return pl.pallas_call(
        functools.partial(_attn_kernel, lam_init=lam_init, tq=tq),
        grid=(b, DA_HEADS, s // tq),
        in_specs=[pl.BlockSpec((4, DA_HEAD), lambda bi, h, qi: (0, 0)),
                  pl.BlockSpec((DA_VDIM, 1), lambda bi, h, qi: (0, 0)),
                  qspec,
                  pl.BlockSpec((None, s, DA_VDIM), lambda bi, h, qi: (bi, 0, h)),
                  pl.BlockSpec((DA_VDIM, s), lambda bi, h, qi: (h, bi))],
        out_specs=qspec,
        out_shape=jax.ShapeDtypeStruct(q.shape, BF16),
        scratch_shapes=[pltpu.VMEM((2, DA_VDIM, tq), F32), pltpu.VMEM((2, 1, tq), F32),
                        pltpu.VMEM((2, 1, tq), F32)],
        compiler_params=pltpu.CompilerParams(dimension_semantics=("arbitrary",) * 3,
                                             vmem_limit_bytes=VMEM_LIMIT_BYTES),
        name="diff_attention",
    )(lam_vecs, subln_g, q, k, vt)


def _out_proj_kernel(a_ref, zc_ref, halo_ref, x_ref, w_ref, dw_ref, dwb_ref, lng_ref, lnb_ref, scw_ref,
                     g2_ref, ones_ref, x1_ref, xn_ref, xnt_ref, ubuf, pbuf, *, tm, tiles_per_seq):
    i = pl.program_id(0)
    keep = jnp.where(i % tiles_per_seq == 0, 0.0, 1.0)
    c1, c2, c3, c4 = CF_WIDTH, 2 * CF_WIDTH, 2 * CF_WIDTH + SC_WIDTH, 2 * CF_WIDTH + 2 * SC_WIDTH

    def glu(z):
        return z[:, :c1] * jax.nn.sigmoid(z[:, c1:c2])

    def cx(z):
        return z[:, c3:c4] * z[:, c4:]

    zc = zc_ref[...]
    halo = halo_ref[...]
    ubuf[0, 0:HALO, :] = glu(halo) * keep
    ubuf[0, HALO:, :] = glu(zc)
    pbuf[0:HALO, :] = cx(halo) * keep
    pbuf[HALO:, :] = cx(zc)
    shifted_rows = tm + HALO - SUBLANES
    for r in range(1, SUBLANES):
        ubuf[r, 0:shifted_rows, :] = ubuf[0, pl.ds(r, shifted_rows), :]

    dw = dw_ref[...]
    u = jnp.zeros((tm, CF_WIDTH), F32) + dwb_ref[...]
    for j in range(CF_CONV):
        off = HALO - (CF_CONV - 1) + j
        u = u + dw[j:j + 1, :] * ubuf[off % SUBLANES, pl.ds(off - off % SUBLANES, tm), :]
    ones_bd = ones_ref[...]
    gsz = CF_WIDTH // CF_GROUPS
    mu = _group_sum(u, ones_bd) * (1.0 / gsz)
    uc = u - mu
    var = _group_sum(uc * uc, ones_bd) * (1.0 / gsz)
    un = uc * lax.rsqrt(var + LN_EPS) * lng_ref[...] + lnb_ref[...]
    un = un * jax.nn.sigmoid(un)

    scw = scw_ref[...]
    cc = jnp.zeros((tm, SC_WIDTH), F32)
    for j in range(SC_CONV):
        cc = cc + scw[j:j + 1, :] * pbuf[pl.ds(HALO - (SC_CONV - 1) + j, tm), :]
    cc = zc[:, c2:c3] * cc

    y = (_dot(a_ref[...], w_ref[0:DA_WIDTH, :])
         + _dot(un.astype(BF16), w_ref[DA_WIDTH:DA_WIDTH + CF_WIDTH, :])
         + _dot(cc.astype(BF16), w_ref[DA_WIDTH + CF_WIDTH:, :]))
    x1 = x_ref[...] + y
    x1_ref[...] = x1
    xn = x1 * lax.rsqrt(jnp.mean(x1 * x1, axis=-1, keepdims=True) + RMS_EPS) * g2_ref[...]
    xn_ref[...] = xn.astype(BF16)
    xnt_ref[...] = xn.T.astype(BF16)


def _out_proj(a, zc, x, w_out_bf, cf_dw, cf_dw_b, cf_ln_g, cf_ln_b, sc_w, g2, *, tm, seq):
    n, d = x.shape
    zc_w = zc.shape[1]
    hb = tm // HALO
    ones_bd = _block_diag_ones(CF_WIDTH, CF_WIDTH // CF_GROUPS)
    full = lambda shape: pl.BlockSpec(shape, lambda i: (0,) * len(shape))
    row = lambda w: pl.BlockSpec((tm, w), lambda i: (i, 0))
    return pl.pallas_call(
        functools.partial(_out_proj_kernel, tm=tm, tiles_per_seq=seq // tm),
        grid=(n // tm,),
        in_specs=[row(DA_WIDTH), row(zc_w),
                  pl.BlockSpec((HALO, zc_w), lambda i: (jnp.maximum(i * hb - 1, 0), 0)),
                  row(d), full((d, d)), full((CF_CONV, CF_WIDTH)), full((1, CF_WIDTH)),
                  full((1, CF_WIDTH)), full((1, CF_WIDTH)), full((SC_CONV, SC_WIDTH)), full((1, d)),
                  full((CF_WIDTH, CF_WIDTH))],
        out_specs=[row(d), row(d), pl.BlockSpec((d, tm), lambda i: (0, i))],
        out_shape=[jax.ShapeDtypeStruct((n, d), F32), jax.ShapeDtypeStruct((n, d), BF16),
                   jax.ShapeDtypeStruct((d, n), BF16)],
        scratch_shapes=[pltpu.VMEM((SUBLANES, tm + HALO, CF_WIDTH), F32), pltpu.VMEM((tm + HALO, SC_WIDTH), F32)],
        compiler_params=pltpu.CompilerParams(dimension_semantics=("arbitrary",),
                                             vmem_limit_bytes=VMEM_LIMIT_BYTES),
        name="out_proj",
    )(a, zc, zc, x, w_out_bf, cf_dw, cf_dw_b, cf_ln_g, cf_ln_b, sc_w, g2, ones_bd)


def _oddeven_merge_sort_pairs(n):
    pairs = []
    p = 1
    while p < n:
        k = p
        while k >= 1:
            for j in range(k % p, n - k, 2 * k):
                for i in range(min(k, n - j - k)):
                    if (i + j) // (2 * p) == (i + j + k) // (2 * p):
                        pairs.append((i + j, i + j + k))
            k //= 2
        p *= 2
    return pairs


def _bitonic_sort_pairs(n):
    pairs = []
    k = n // 2
    while k >= 1:
        for i in range(n):
            if (i // k) % 2 == 0:
                pairs.append((i, i + k))
        k //= 2
    return pairs


SORT16_PAIRS = _oddeven_merge_sort_pairs(PEER_TOPK)
BITONIC16_PAIRS = _bitonic_sort_pairs(PEER_TOPK)
CAND_CELLS = [(j, k) for j in range(PEER_TOPK) for k in range(PEER_TOPK) if (j + 1) * (k + 1) <= PEER_TOPK]


def _beats(a, b):
    return (a[0] > b[0]) | ((a[0] == b[0]) & (a[1] < b[1]))


def _cmp_exchange(items, i, j, exact):
    a, b = items[i], items[j]
    if exact:
        sw = _beats(b, a)
        items[i] = (jnp.where(sw, b[0], a[0]), jnp.where(sw, b[1], a[1]))
        items[j] = (jnp.where(sw, a[0], b[0]), jnp.where(sw, a[1], b[1]))
    else:
        sw = b[0] > a[0]
        items[i] = (jnp.maximum(a[0], b[0]), jnp.where(sw, b[1], a[1]))
        items[j] = (jnp.minimum(a[0], b[0]), jnp.where(sw, a[1], b[1]))


def _merge_top(a, b, exact):
    out = []
    for i in range(PEER_TOPK):
        x, y = a[i], b[PEER_TOPK - 1 - i]
        if exact:
            sw = _beats(y, x)
            out.append((jnp.where(sw, y[0], x[0]), jnp.where(sw, y[1], x[1])))
        else:
            out.append((jnp.maximum(x[0], y[0]), jnp.where(y[0] > x[0], y[1], x[1])))
    for (i, j) in BITONIC16_PAIRS:
        _cmp_exchange(out, i, j, exact)
    return out


def _peer_route_kernel(xn_ref, wq_ref, keys_ref, r2_ref, e2_ref, n1_ref, e1_ref,
                       sbuf, sval, sidx, topv, topi, obuf, *, tr):
    ng = tr // LANES
    assert ng == SUBLANES
    q = _dot(xn_ref[...], wq_ref[...]).astype(BF16)
    keys = keys_ref[...].astype(BF16)
    for c in range(2):
        st = _dot_nt(keys[c], q[:, c * PEER_HALF:(c + 1) * PEER_HALF])
        for g in range(ng):
            sbuf[c, g * KEY_PITCH:g * KEY_PITCH + N_KEYS, :] = st[:, g * LANES:(g + 1) * LANES]

    def key_row(c, n):
        return sbuf[c, pl.ds(n, ng, stride=KEY_PITCH), :]

    n_groups = N_KEYS // PEER_TOPK

    def top_lists(exact):
        def sort_group(t, carry):
            c = t // n_groups
            grp = t % n_groups
            base = grp * PEER_TOPK
            items = []
            for r in range(PEER_TOPK):
                val = key_row(c, base + r)
                idx = jnp.full(val.shape, r, jnp.int32) + base
                items.append((val, idx))
            for (i, j) in SORT16_PAIRS:
                _cmp_exchange(items, i, j, exact)
            for r in range(PEER_TOPK):
                sval[t, r] = items[r][0]
                sidx[t, r] = items[r][1]
            return carry

        lax.fori_loop(0, 2 * n_groups, sort_group, 0)

        for c in range(2):
            lists = [[(sval[c * n_groups + g, r], sidx[c * n_groups + g, r]) for r in range(PEER_TOPK)]
                     for g in range(n_groups)]
            while len(lists) > 1:
                lists = [_merge_top(lists[2 * m], lists[2 * m + 1], exact) for m in range(len(lists) // 2)]
            for r in range(PEER_TOPK):
                topv[c, r] = lists[0][r][0]
                topi[c, r] = lists[0][r][1]

    top_lists(exact=False)
    doubt = jnp.zeros((ng, LANES), F32)
    for c in range(2):
        kept = [topv[c, r] for r in range(PEER_TOPK)]
        for r in range(PEER_TOPK - 1):
            doubt = jnp.maximum(doubt, jnp.where(kept[r] == kept[r + 1], 1.0, 0.0))
        at_least = sum(jnp.where(key_row(c, n) >= kept[PEER_TOPK - 1], 1.0, 0.0) for n in range(N_KEYS))
        doubt = jnp.maximum(doubt, jnp.where(at_least != float(PEER_TOPK), 1.0, 0.0))

    @pl.when(jnp.max(doubt) > 0.0)
    def _():
        top_lists(exact=True)

    (v1, i1), (v2, i2) = [([topv[c, r] for r in range(PEER_TOPK)], [topi[c, r] for r in range(PEER_TOPK)])
                          for c in range(2)]

    cand = {cell: v1[cell[0]] + v2[cell[1]] for cell in CAND_CELLS}
    rank = {cell: jnp.zeros(v1[0].shape, F32) for cell in CAND_CELLS}
    for pi, p in enumerate(CAND_CELLS):
        for qcell in CAND_CELLS[pi + 1:]:
            if p[0] <= qcell[0] and p[1] <= qcell[1]:
                rank[qcell] = rank[qcell] + 1.0
            else:
                ge = jnp.where(cand[p] >= cand[qcell], 1.0, 0.0)
                rank[qcell] = rank[qcell] + ge
                rank[p] = rank[p] + (1.0 - ge)
    sel = {cell: jnp.where(rank[cell] < PEER_TOPK, 1.0, 0.0) for cell in CAND_CELLS}
    cnt = [sum(sel[(j, k)] for k in range(PEER_TOPK) if (j, k) in sel) for j in range(PEER_TOPK)]

    ej = [jnp.exp(v1[j] - v1[0]) for j in range(PEER_TOPK)]
    ek = [jnp.exp(v2[k] - v2[0]) for k in range(PEER_TOPK)]
    zsum = sum(sel[(j, k)] * (ej[j] * ek[k]) for (j, k) in CAND_CELLS)
    zinv = 1.0 / zsum

    def emit(out_ref, rows):
        for n in range(N_KEYS):
            obuf[pl.ds(n, ng, stride=KEY_PITCH), :] = rows(n)
        for g in range(ng):
            out_ref[:, g * LANES:(g + 1) * LANES] = obuf[g * KEY_PITCH:g * KEY_PITCH + N_KEYS, :].astype(out_ref.dtype)

    def rank2_row(n):
        acc = jnp.full(v1[0].shape, float(PEER_TOPK), F32)
        for k in range(PEER_TOPK):
            acc = jnp.where(i2[k] == n, float(k), acc)
        return acc

    def count1_row(n):
        acc = jnp.zeros(v1[0].shape, F32)
        for j in range(PEER_TOPK):
            acc = jnp.where(i1[j] == n, cnt[j], acc)
        return acc

    emit(r2_ref, rank2_row)
    emit(n1_ref, count1_row)
    emit(e2_ref, lambda n: jnp.exp(key_row(1, n) - v2[0]))
    emit(e1_ref, lambda n: jnp.exp(key_row(0, n) - v1[0]) * zinv)


def _peer_route(xn, wq_bf, keys, *, tr):
    n, d = xn.shape
    hq = wq_bf.shape[1] // PEER_HEADS
    ng = tr // LANES
    mapspec = pl.BlockSpec((None, N_KEYS, tr), lambda i, h: (h, 0, i))
    mapshapes = [jax.ShapeDtypeStruct((PEER_HEADS, N_KEYS, n), dt) for dt in (BF16, BF16, F32, F32)]
    n_lists = 2 * (N_KEYS // PEER_TOPK)
    return pl.pallas_call(
        functools.partial(_peer_route_kernel, tr=tr),
        grid=(n // tr, PEER_HEADS),
        in_specs=[pl.BlockSpec((tr, d), lambda i, h: (i, 0)),
                  pl.BlockSpec((d, hq), lambda i, h: (0, h)),
                  pl.BlockSpec((None, 2, N_KEYS, PEER_HALF), lambda i, h: (h, 0, 0, 0))],
        out_specs=[mapspec] * 4,
        out_shape=mapshapes,
        scratch_shapes=[pltpu.VMEM((2, ng * KEY_PITCH, LANES), F32),
                        pltpu.VMEM((n_lists, PEER_TOPK, ng, LANES), F32),
                        pltpu.VMEM((n_lists, PEER_TOPK, ng, LANES), jnp.int32),
                        pltpu.VMEM((2, PEER_TOPK, ng, LANES), F32),
                        pltpu.VMEM((2, PEER_TOPK, ng, LANES), jnp.int32),
                        pltpu.VMEM((ng * KEY_PITCH, LANES), F32)],
        compiler_params=pltpu.CompilerParams(dimension_semantics=("arbitrary",) * 2,
                                             vmem_limit_bytes=VMEM_LIMIT_BYTES),
        name="peer_route",
    )(xn, wq_bf, keys)


def _gelu_x2(x):
    return x * (1.0 + lax.erf(x * (2.0 ** -0.5)))


BF16_ROWS = 2 * SUBLANES


def _peer_dense_kernel(xnt_ref, u_ref, vt_ref, r2_ref, e2_ref, n1_ref, e1_ref, x1_ref, o_ref,
                       acc_ref, act_ref, w_ref, *, et, tc):
    e = pl.program_id(1)
    ne = pl.num_programs(1)
    a_per = et // N_KEYS
    tt = xnt_ref.shape[1]
    n_chunks = tt // tc
    packed = (N_KEYS // BF16_ROWS, BF16_ROWS, tc)

    @pl.when(e == 0)
    def _():
        acc_ref[...] = jnp.zeros_like(acc_ref)

    def first_key_row(ref, h, arow, cols):
        row = ref[h, pl.ds(arow, 1), cols]
        return jnp.broadcast_to(row, (BF16_ROWS, tc)).astype(BF16)[None]

    def pre_activation(c):
        return _dot(u_ref[...], xnt_ref[:, c * tc:(c + 1) * tc])

    act_ref[0] = pre_activation(0)
    for c in range(n_chunks):
        cols = slice(c * tc, (c + 1) * tc)
        if c + 1 < n_chunks:
            act_ref[(c + 1) % 2] = pre_activation(c + 1)
        for al in range(a_per):
            arow = e * a_per + al
            rows = slice(al * N_KEYS, (al + 1) * N_KEYS)
            gate = jnp.zeros(packed, BF16)
            for h in range(PEER_HEADS):
                n1 = first_key_row(n1_ref, h, arow, cols)
                e1 = first_key_row(e1_ref, h, arow, cols)
                r2 = r2_ref[h, :, cols].reshape(packed)
                e2 = e2_ref[h, :, cols].reshape(packed)
                gate = gate + jnp.where(r2 < n1, e2, jnp.zeros((), BF16)) * e1
            g = _gelu_x2(act_ref[c % 2, rows, :]).astype(BF16)
            w_ref[c, rows, :] = g * gate.reshape(N_KEYS, tc)
        acc_ref[:, cols] += _dot(vt_ref[...], w_ref[c])

    @pl.when(e == ne - 1)
    def _():
        o_ref[...] = x1_ref[...] + acc_ref[...].T


def _peer_dense(xnt, u_bf, vt_bf, maps, x1, *, tt, et, tc):
    d, n = xnt.shape
    n_exp = u_bf.shape[0]
    once = pl.Buffered(1)
    mapspec = pl.BlockSpec((PEER_HEADS, N_KEYS, tt), lambda i, e: (0, 0, i), pipeline_mode=once)
    mapspec2 = pl.BlockSpec((PEER_HEADS, N_KEYS, tt), lambda i, e: (0, 0, i))
    return pl.pallas_call(
        functools.partial(_peer_dense_kernel, et=et, tc=tc),
        grid=(n // tt, n_exp // et),
        in_specs=[pl.BlockSpec((d, tt), lambda i, e: (0, i)),
                  pl.BlockSpec((et, d), lambda i, e: (e, 0)),
                  pl.BlockSpec((d, et), lambda i, e: (0, e)),
                  mapspec2, mapspec2, mapspec, mapspec,
                  pl.BlockSpec((tt, d), lambda i, e: (i, 0), pipeline_mode=once)],
        out_specs=pl.BlockSpec((tt, d), lambda i, e: (i, 0)),
        out_shape=jax.ShapeDtypeStruct((n, d), F32),
        scratch_shapes=[pltpu.VMEM((d, tt), F32), pltpu.VMEM((2, et, tc), F32),
                        pltpu.VMEM((tt // tc, et, tc), BF16)],
        compiler_params=pltpu.CompilerParams(dimension_semantics=("arbitrary",) * 2,
                                             vmem_limit_bytes=VMEM_LIMIT_BYTES),
        name="peer_dense",
    )(xnt, u_bf, vt_bf, *maps, x1)


def _tiles(n_tokens, seq):
    tm = min(512, seq)
    tq = min(512, seq)
    tr = SUBLANES * LANES
    tt = min(1024, n_tokens)
    et = 1024
    tc = 256
    return tm, tq, tr, tt, et, tc


def _layer(x2d, batch, seq, lam_init, p):
    n, d = x2d.shape
    tm, tq, tr, tt, et, tc = _tiles(n, seq)
    row = lambda v: v.reshape(1, -1)
    q, k, vt, zc = _in_proj(x2d, row(p["norm1_g"]), p["w_in"].astype(BF16),
                            row(jnp.tile(p["q_norm_g"], DA_WIDTH // DA_HEAD)),
                            row(jnp.tile(p["k_norm_g"], DA_WIDTH // DA_HEAD)), tm=tm)
    lam_vecs = jnp.stack([p["lam_q1"], p["lam_k1"], p["lam_q2"], p["lam_k2"]])
    shp = (batch, seq, DA_WIDTH)
    a = _attention(q.reshape(shp), k.reshape(shp), vt, lam_vecs, p["subln_g"].reshape(-1, 1),
                   lam_init=lam_init, tq=tq).reshape(n, DA_WIDTH)
    x1, xn, xnt = _out_proj(a, zc, x2d, p["w_out"].astype(BF16), p["cf_dw"], row(p["cf_dw_b"]),
                            row(p["cf_ln_g"]), row(p["cf_ln_b"]), p["sc_w"], row(p["norm2_g"]), tm=tm, seq=seq)
    maps = _peer_route(xn, p["peer_wq"].astype(BF16), p["peer_keys"], tr=tr)
    half_vt = (0.5 * p["peer_v"]).astype(BF16).T
    return _peer_dense(xnt, p["peer_u"].astype(BF16), half_vt, maps, x1,
                       tt=tt, et=et, tc=tc)


def kernel(x, norm1_g, w_in, q_norm_g, k_norm_g, lam_q1, lam_k1, lam_q2, lam_k2, subln_g, cf_dw, cf_dw_b,
           cf_ln_g, cf_ln_b, sc_w, w_out, norm2_g, peer_wq, peer_keys, peer_u, peer_v):
    params = dict(norm1_g=norm1_g, w_in=w_in, q_norm_g=q_norm_g, k_norm_g=k_norm_g, lam_q1=lam_q1,
                  lam_k1=lam_k1, lam_q2=lam_q2, lam_k2=lam_k2, subln_g=subln_g, cf_dw=cf_dw,
                  cf_dw_b=cf_dw_b, cf_ln_g=cf_ln_g, cf_ln_b=cf_ln_b, sc_w=sc_w, w_out=w_out,
                  norm2_g=norm2_g, peer_wq=peer_wq, peer_keys=peer_keys, peer_u=peer_u, peer_v=peer_v)
    batch, seq, d = x.shape
    x2d = x.reshape(batch * seq, d)
    for l in range(DEPTH):
        lam_init = 0.8 - 0.6 * math.exp(-0.3 * l)
        x2d = _layer(x2d, batch, seq, lam_init, {name: val[l] for name, val in params.items()})
    return x2d.reshape(batch, seq, d)
```
